```python
import jax, jax.numpy as jnp
from jax import lax
import numpy as np

D_MODEL = 2048
BATCH = 2
SEQ = 16384
DEPTH = 1

CHUNK = 64
Q_BLOCK = 128
N_MEM = 256
EPS = 1e-6

MLA_V = 128
MLA_NOPE = 128
MLA_ROPE = 64
MLA_HEADS = (D_MODEL // 2) // MLA_V
MLA_Q_RANK = 512
MLA_KV_RANK = 256
ROPE_THETA = 10000.0

HG_DK = 128
HG_DV = 128
HG_HEADS = (D_MODEL // 2) // HG_DV

MLA_WIDTH = MLA_HEADS * MLA_V
HG_WIDTH = HG_HEADS * HG_DV
MIX_WIDTH = MLA_WIDTH + HG_WIDTH
IN_SPLITS = (MLA_Q_RANK, MLA_KV_RANK, MLA_ROPE, HG_HEADS * HG_DK, HG_HEADS * HG_DK, HG_WIDTH, HG_WIDTH)
IN_WIDTH = sum(IN_SPLITS)

X_HEADS = 4
X_DIM = D_MODEL // X_HEADS

D_FF = 11 * D_MODEL // 4
CONV_WIDTH = 3

kernel_name = "hybrid_mla_hgrn2_streaming_layer"


def rms_norm(x, g):
    xf = x.astype(jnp.float32)
    y = xf * lax.rsqrt(jnp.mean(xf * xf, axis=-1, keepdims=True) + EPS)
    return (y * g.astype(jnp.float32)).astype(x.dtype)


def apply_rope(x, cos, sin):
    xf = x.astype(jnp.float32)
    x1, x2 = jnp.split(xf, 2, axis=-1)
    return jnp.concatenate([x1 * cos - x2 * sin, x2 * cos + x1 * sin], axis=-1).astype(x.dtype)


def chunk_causal_attention(q, k, v, scale):
    B, S, H, Dq = q.shape
    nb = S // Q_BLOCK
    qb = q.reshape(B, nb, Q_BLOCK, H, Dq).transpose(1, 0, 2, 3, 4)
    key_chunk = jnp.arange(S) // CHUNK
    neg = jnp.finfo(jnp.float32).min

    def one_block(args):
        qi, i = args
        q_chunk = (i * Q_BLOCK + jnp.arange(Q_BLOCK)) // CHUNK
        mask = key_chunk[None, :] <= q_chunk[:, None]
        s = jnp.einsum('bqhd,bkhd->bhqk', qi, k, preferred_element_type=jnp.float32) * scale
        s = jnp.where(mask[None, None], s, neg)
        p = jax.nn.softmax(s, axis=-1).astype(v.dtype)
        return jnp.einsum('bhqk,bkhd->bqhd', p, v)

    o = lax.map(one_block, (qb, jnp.arange(nb)))
    return o.transpose(1, 0, 2, 3, 4).reshape(B, S, H, v.shape[-1])


def mla_group(c_q, c_kv, k_r, q_norm, w_uq, kv_norm, w_ukv, cos, sin):
    B, S, _ = c_q.shape
    q = (rms_norm(c_q, q_norm) @ w_uq).reshape(B, S, MLA_HEADS, MLA_NOPE + MLA_ROPE)
    q_nope, q_rope = q[..., :MLA_NOPE], q[..., MLA_NOPE:]
    q_rope = apply_rope(q_rope, cos[None, :, None, :], sin[None, :, None, :])
    kv = (rms_norm(c_kv, kv_norm) @ w_ukv).reshape(B, S, MLA_HEADS, MLA_NOPE + MLA_V)
    k_nope, v = kv[..., :MLA_NOPE], kv[..., MLA_NOPE:]
    k_rope = apply_rope(k_r, cos[None], sin[None])
    k_rope = jnp.broadcast_to(k_rope[:, :, None, :], (B, S, MLA_HEADS, MLA_ROPE))
    qf = jnp.concatenate([q_nope, q_rope], axis=-1)
    kf = jnp.concatenate([k_nope, k_rope], axis=-1)
    scale = (MLA_NOPE + MLA_ROPE) ** -0.5
    o = chunk_causal_attention(qf, kf, v, scale)
    return o.reshape(B, S, MLA_WIDTH)


def hgrn2_group(hq, hf, hi, hg, lb, out_gain):
    B, S, _ = hq.shape
    f32 = jnp.float32
    q = jax.nn.silu(hq.astype(f32))
    f = lb + (1.0 - lb) * jax.nn.sigmoid(hf.astype(f32))
    k = 1.0 - f
    logf = jnp.log(f)
    v = hi.astype(f32)
    nc = S // CHUNK

    def to_chunks(t, d):
        return t.reshape(B, nc, CHUNK, HG_HEADS, d).transpose(1, 0, 3, 2, 4)

    qc, kc, gc = to_chunks(q, HG_DK), to_chunks(k, HG_DK), to_chunks(logf, HG_DK)
    vc = to_chunks(v, HG_DV)
    tri = jnp.tril(jnp.ones((CHUNK, CHUNK), dtype=bool))

    def body(state, inp):
        qi, ki, vi, gi = inp
        b = jnp.cumsum(gi, axis=2)
        diff = b[:, :, :, None, :] - b[:, :, None, :, :]
        decay = jnp.exp(jnp.where(tri[None, None, :, :, None], diff, -jnp.inf))
        a = jnp.einsum('bhtd,bhsd,bhtsd->bhts', qi, ki, decay)
        o = jnp.einsum('bhts,bhsv->bhtv', a, vi) + jnp.einsum('bhtd,bhdv->bhtv', qi * jnp.exp(b), state)
        b_last = b[:, :, -1:, :]
        state = jnp.exp(b_last[:, :, 0, :])[..., None] * state + jnp.einsum(
            'bhsd,bhsv->bhdv', ki * jnp.exp(b_last - b), vi)
        return state, o

    s0 = jnp.zeros((B, HG_HEADS, HG_DK, HG_DV), f32)
    _, o = lax.scan(body, s0, (qc, kc, vc, gc))
    o = o.transpose(1, 0, 3, 2, 4).reshape(B, S, HG_HEADS, HG_DV)
    o = rms_norm(o, out_gain) * jax.nn.silu(hg.astype(f32).reshape(B, S, HG_HEADS, HG_DV))
    return o.reshape(B, S, HG_WIDTH).astype(hq.dtype)


def causal_dwconv(u, w, b):
    C = u.shape[-1]
    y = lax.conv_general_dilated(u, w[:, None, :].astype(u.dtype), window_strides=(1,),
                                 padding=[(CONV_WIDTH - 1, 0)],
                                 dimension_numbers=('NWC', 'WIO', 'NWC'),
                                 feature_group_count=C)
    return y + b.astype(u.dtype)


def setup_inputs(seed: int = 0) -> dict:
    key = jax.random.key(seed)
    ks = jax.random.split(key, 32)
    f32 = jnp.float32

    def w(k, shape, fan_in):
        return jax.random.normal(k, shape, f32) * (fan_in ** -0.5)

    def gain(k, shape):
        return 1.0 + 0.02 * jax.random.normal(k, shape, f32)

    L = DEPTH
    return {
        "x": jax.random.normal(ks[0], (BATCH, SEQ, D_MODEL), f32),
        "mem": jax.random.normal(ks[1], (BATCH, N_MEM, D_MODEL), f32),
        "w_in": w(ks[2], (L, D_MODEL, IN_WIDTH), D_MODEL),
        "q_norm": gain(ks[3], (L, MLA_Q_RANK)),
        "w_uq": w(ks[4], (L, MLA_Q_RANK, MLA_HEADS * (MLA_NOPE + MLA_ROPE)), MLA_Q_RANK),
        "kv_norm": gain(ks[5], (L, MLA_KV_RANK)),
        "w_ukv": w(ks[6], (L, MLA_KV_RANK, MLA_HEADS * (MLA_NOPE + MLA_V)), MLA_KV_RANK),
        "mla_out_norm": gain(ks[7], (L, MLA_WIDTH)),
        "hgrn_lb": 0.5 * jax.random.normal(ks[8], (L + 1, HG_HEADS * HG_DK), f32),
        "hgrn_out_norm": gain(ks[9], (L, HG_DV)),
        "w_out": w(ks[10], (L, MIX_WIDTH, D_MODEL), MIX_WIDTH),
        "ln_mix_pre": gain(ks[11], (L, D_MODEL)),
        "ln_mix_post": gain(ks[12], (L, D_MODEL)),
        "ln_x_pre": gain(ks[13], (L, D_MODEL)),
        "ln_x_post": gain(ks[14], (L, D_MODEL)),
        "mem_norm": gain(ks[15], (L, D_MODEL)),
        "w_xq": w(ks[16], (L, D_MODEL, D_MODEL), D_MODEL),
        "w_xk": w(ks[17], (L, D_MODEL, D_MODEL), D_MODEL),
        "w_xv": w(ks[18], (L, D_MODEL, D_MODEL), D_MODEL),
        "w_xo": w(ks[19], (L, D_MODEL, D_MODEL), D_MODEL),
        "ln_ffn_pre": gain(ks[20], (L, D_MODEL)),
        "ln_ffn_post": gain(ks[21], (L, D_MODEL)),
        "w_up": w(ks[22], (L, D_MODEL, 2 * D_FF), D_MODEL),
        "conv_w": w(ks[23], (L, CONV_WIDTH, 2 * D_FF), CONV_WIDTH),
        "conv_b": 0.01 * jax.random.normal(ks[24], (L, 2 * D_FF), f32),
        "w_down": w(ks[25], (L, D_FF, D_MODEL), D_FF),
    }


def reference(x, mem, w_in, q_norm, w_uq, kv_norm, w_ukv, mla_out_norm, hgrn_lb, hgrn_out_norm,
              w_out, ln_mix_pre, ln_mix_post, ln_x_pre, ln_x_post, mem_norm, w_xq, w_xk, w_xv, w_xo,
              ln_ffn_pre, ln_ffn_post, w_up, conv_w, conv_b, w_down):
    B, S, D = x.shape
    M = mem.shape[1]
    f32 = jnp.float32
    split_idx = np.cumsum(np.array(IN_SPLITS))[:-1].tolist()

    pos = jnp.arange(S, dtype=f32)
    inv_freq = 1.0 / (ROPE_THETA ** (jnp.arange(0, MLA_ROPE, 2, dtype=f32) / MLA_ROPE))
    ang = pos[:, None] * inv_freq[None, :]
    cos, sin = jnp.cos(ang), jnp.sin(ang)

    lb_all = jnp.cumsum(jax.nn.softmax(hgrn_lb.astype(f32), axis=0), axis=0)

    h = x
    for l in range(DEPTH):
        xn = rms_norm(h, ln_mix_pre[l])
        z = xn @ w_in[l]
        c_q, c_kv, k_r, hq, hf, hi, hg = jnp.split(z, split_idx, axis=-1)
        a = mla_group(c_q, c_kv, k_r, q_norm[l], w_uq[l], kv_norm[l], w_ukv[l], cos, sin)
        a = rms_norm(a, mla_out_norm[l])
        r = hgrn2_group(hq, hf, hi, hg, lb_all[l], hgrn_out_norm[l])
        y = jnp.concatenate([a, r], axis=-1) @ w_out[l]
        h = h + rms_norm(y, ln_mix_post[l])

        xn = rms_norm(h, ln_x_pre[l])
        mn = rms_norm(mem, mem_norm[l])
        qx = (xn @ w_xq[l]).reshape(B, S, X_HEADS, X_DIM)
        kx = (mn @ w_xk[l]).reshape(B, M, X_HEADS, X_DIM)
        vx = (mn @ w_xv[l]).reshape(B, M, X_HEADS, X_DIM)
        s = jnp.einsum('bqhd,bkhd->bhqk', qx, kx, preferred_element_type=f32) * (X_DIM ** -0.5)
        p = jax.nn.softmax(s, axis=-1).astype(vx.dtype)
        ox = jnp.einsum('bhqk,bkhd->bqhd', p, vx).reshape(B, S, D) @ w_xo[l]
        h = h + rms_norm(ox, ln_x_post[l])

        xn = rms_norm(h, ln_ffn_pre[l])
        u = causal_dwconv(xn @ w_up[l], conv_w[l], conv_b[l])
        gate, val = jnp.split(u, 2, axis=-1)
        yf = (jax.nn.gelu(gate, approximate=True) * val) @ w_down[l]
        h = h + rms_norm(yf, ln_ffn_post[l])
    return h
```

```python
import functools
import math

import jax
import jax.numpy as jnp
from jax import lax
from jax.experimental import pallas as pl
from jax.experimental.pallas import tpu as pltpu

F32 = jnp.float32
BF16 = jnp.bfloat16

EPS = 1e-6
CHUNK = 64
MLA_V = 128
MLA_NOPE = 128
MLA_ROPE = 64
MLA_HEADS = 8
MLA_QK = MLA_NOPE + MLA_ROPE
MLA_Q_RANK = 512
MLA_KV_RANK = 256
ROPE_THETA = 10000.0
HG_DK = 128
HG_DV = 128
HG_HEADS = 8
HG_WIDTH = HG_HEADS * HG_DV
MLA_WIDTH = MLA_HEADS * MLA_V
X_HEADS = 4
CONV_WIDTH = 3

LOG2E = 1.4426950408889634
NEG = -1e30

VMEM_LIMIT_BYTES = 56 * 1024 * 1024

ROW_TILE = 512
ATT_BLK = 512
HG_BLK = 256
HG_SUB = 16
FF_TILE = 512
CONV_PAD = 8


def _params(*sem):
    return pltpu.CompilerParams(dimension_semantics=sem, vmem_limit_bytes=VMEM_LIMIT_BYTES)


def _rms(xf, g):
    return xf * lax.rsqrt(jnp.mean(xf * xf, axis=-1, keepdims=True) + EPS) * g


def _const_spec(shape):
    nd = len(shape)
    return pl.BlockSpec(shape, lambda *_: (0,) * nd, pipeline_mode=pl.Buffered(1))


def _mla_prep_body(x_ref, g_ref, wm_ref, qn_ref, wuqT_ref, kvn_ref, wuk_ref, wuvT_ref,
                   cos_ref, sin_ref, cosT_ref, sinT_ref, qT_ref, k_ref, vT_ref, *, q_scale):
    xn = _rms(x_ref[0], g_ref[...]).astype(BF16)
    z = jnp.dot(xn, wm_ref[...], preferred_element_type=F32)
    cq = _rms(z[:, :MLA_Q_RANK], qn_ref[...]).astype(BF16)
    ckv = _rms(z[:, MLA_Q_RANK:MLA_Q_RANK + MLA_KV_RANK], kvn_ref[...]).astype(BF16)
    kr = z[:, MLA_Q_RANK + MLA_KV_RANK:]

    nt = (((1,), (1,)), ((), ()))
    qT = lax.dot_general(wuqT_ref[...], cq, nt, preferred_element_type=F32)
    cosT = cosT_ref[...]
    sinT = sinT_ref[...]
    half = MLA_ROPE // 2
    for h in range(MLA_HEADS):
        base = h * MLA_QK
        x1 = qT[base + MLA_NOPE:base + MLA_NOPE + half]
        x2 = qT[base + MLA_NOPE + half:base + MLA_QK]
        qT_ref[0, base:base + MLA_NOPE, :] = (qT[base:base + MLA_NOPE] * q_scale).astype(BF16)
        qT_ref[0, base + MLA_NOPE:base + MLA_NOPE + half, :] = ((x1 * cosT - x2 * sinT) * q_scale).astype(BF16)
        qT_ref[0, base + MLA_NOPE + half:base + MLA_QK, :] = ((x2 * cosT + x1 * sinT) * q_scale).astype(BF16)

    kn = jnp.dot(ckv, wuk_ref[...], preferred_element_type=F32)
    cos = cos_ref[...]
    sin = sin_ref[...]
    k1 = kr[:, :half]
    k2 = kr[:, half:]
    krope = jnp.concatenate([k1 * cos - k2 * sin, k2 * cos + k1 * sin], axis=-1).astype(BF16)
    for h in range(MLA_HEADS):
        k_ref[0, h, :, 0:MLA_NOPE] = kn[:, h * MLA_NOPE:(h + 1) * MLA_NOPE].astype(BF16)
        k_ref[0, h, :, MLA_NOPE:MLA_QK] = krope

    vT = lax.dot_general(wuvT_ref[...], ckv, nt, preferred_element_type=F32)
    for h in range(MLA_HEADS):
        vT_ref[0, h, 0] = vT[h * MLA_V:(h + 1) * MLA_V].astype(BF16)


def _mla_prep(x, g, wm, qn, wuqT, kvn, wuk, wuvT, cos, sin, cosT, sinT, blk):
    B, S, D = x.shape
    nb = S // blk
    half = MLA_ROPE // 2
    q_scale = (MLA_QK ** -0.5) * LOG2E
    return pl.pallas_call(
        functools.partial(_mla_prep_body, q_scale=q_scale),
        grid=(B, nb),
        in_specs=[
            pl.BlockSpec((1, blk, D), lambda b, i: (b, i, 0)),
            _const_spec(g.shape), _const_spec(wm.shape), _const_spec(qn.shape), _const_spec(wuqT.shape),
            _const_spec(kvn.shape), _const_spec(wuk.shape), _const_spec(wuvT.shape),
            pl.BlockSpec((blk, half), lambda b, i: (i, 0)),
            pl.BlockSpec((blk, half), lambda b, i: (i, 0)),
            pl.BlockSpec((half, blk), lambda b, i: (0, i)),
            pl.BlockSpec((half, blk), lambda b, i: (0, i)),
        ],
        out_specs=[
            pl.BlockSpec((1, MLA_HEADS * MLA_QK, blk), lambda b, i: (b, 0, i)),
            pl.BlockSpec((1, MLA_HEADS, blk, MLA_QK), lambda b, i: (b, 0, i, 0)),
            pl.BlockSpec((1, MLA_HEADS, 1, MLA_V, blk), lambda b, i: (b, 0, i, 0, 0)),
        ],
        out_shape=[
            jax.ShapeDtypeStruct((B, MLA_HEADS * MLA_QK, S), BF16),
            jax.ShapeDtypeStruct((B, MLA_HEADS, S, MLA_QK), BF16),
            jax.ShapeDtypeStruct((B, MLA_HEADS, nb, MLA_V, blk), BF16),
        ],
        compiler_params=_params("parallel", "parallel"),
        name="mla_prep",
    )(x, g, wm, qn, wuqT, kvn, wuk, wuvT, cos, sin, cosT, sinT)


def _attn_body(qT_ref, k_ref, vT_ref, o_ref, m_ref, l_ref, acc_ref, *, blk):
    i = pl.program_id(2)
    qT = qT_ref[0]
    m_ref[...] = jnp.full(m_ref.shape, NEG, F32)
    l_ref[...] = jnp.zeros(l_ref.shape, F32)
    acc_ref[...] = jnp.zeros(acc_ref.shape, F32)

    def update(j, mask):
        kblk = k_ref[0, 0, pl.ds(pl.multiple_of(j * blk, blk), blk), :]
        s = jnp.dot(kblk, qT, preferred_element_type=F32)
        if mask is not None:
            s = jnp.where(mask, s, NEG)
        m_prev = m_ref[...]
        m_new = jnp.maximum(m_prev, jnp.max(s, axis=0, keepdims=True))
        alpha = jnp.exp2(m_prev - m_new)
        p = jnp.exp2(s - m_new)
        l_ref[...] = alpha * l_ref[...] + jnp.sum(p, axis=0, keepdims=True)
        acc_ref[...] = alpha * acc_ref[...] + jnp.dot(vT_ref[0, 0, j], p.astype(BF16),
                                                      preferred_element_type=F32)
        m_ref[...] = m_new

    def body(j, carry):
        update(j, None)
        return carry

    lax.fori_loop(0, i, body, 0)
    key_chunk = lax.broadcasted_iota(jnp.int32, (blk, blk), 0) // CHUNK
    qry_chunk = lax.broadcasted_iota(jnp.int32, (blk, blk), 1) // CHUNK
    update(i, key_chunk <= qry_chunk)
    o_ref[0] = (acc_ref[...] / l_ref[...]).T


def _attention(qT, k, vT, blk):
    B, H, S, _ = k.shape
    nb = S // blk
    return pl.pallas_call(
        functools.partial(_attn_body, blk=blk),
        grid=(B, H, nb),
        in_specs=[
            pl.BlockSpec((1, MLA_QK, blk), lambda b, h, i: (b, h, i)),
            pl.BlockSpec((1, 1, S, MLA_QK), lambda b, h, i: (b, h, 0, 0)),
            pl.BlockSpec((1, 1, nb, MLA_V, blk), lambda b, h, i: (b, h, 0, 0, 0)),
        ],
        out_specs=pl.BlockSpec((1, blk, MLA_V), lambda b, h, i: (b, i, h)),
        out_shape=jax.ShapeDtypeStruct((B, S, H * MLA_V), F32),
        scratch_shapes=[pltpu.VMEM((1, blk), F32), pltpu.VMEM((1, blk), F32), pltpu.VMEM((MLA_V, blk), F32)],
        compiler_params=_params("parallel", "parallel", "arbitrary"),
        name="mla_attention",
    )(qT, k, vT)


def _norm_matmul_body(x_ref, g_ref, w_ref, o_ref, xn_ref):
    @pl.when(pl.program_id(1) == 0)
    def _():
        xn_ref[...] = _rms(x_ref[...], g_ref[...]).astype(BF16)

    o_ref[...] = jnp.dot(xn_ref[...], w_ref[...], preferred_element_type=F32).astype(o_ref.dtype)


def _norm_matmul(x2d, g, w, out_dtype, tm, tn, name):
    N, D = x2d.shape
    M = w.shape[1]
    return pl.pallas_call(
        _norm_matmul_body,
        grid=(N // tm, M // tn),
        in_specs=[
            pl.BlockSpec((tm, D), lambda i, j: (i, 0)),
            _const_spec(g.shape),
            pl.BlockSpec((D, tn), lambda i, j: (0, j)),
        ],
        out_specs=pl.BlockSpec((tm, tn), lambda i, j: (i, j)),
        out_shape=jax.ShapeDtypeStruct((N, M), out_dtype),
        scratch_shapes=[pltpu.VMEM((tm, D), BF16)],
        compiler_params=_params("parallel", "arbitrary"),
        name=name,
    )(x2d, g, w)


def _hgrn_body(hq_ref, hf_ref, hi_ref, hg_ref, lbp_ref, gain_ref, o_ref,
               st_ref, q_s, k_s, b_s, v_s, qd_s, kd_s, ds_s, *, lc, layer):
    T = HG_SUB

    @pl.when(pl.program_id(2) == 0)
    def _():
        st_ref[...] = jnp.zeros(st_ref.shape, F32)

    lbp = lbp_ref[...]
    e = jnp.exp(lbp - jnp.max(lbp, axis=0, keepdims=True))
    prob = e / jnp.sum(e, axis=0, keepdims=True)
    lb = jnp.sum(prob[:layer + 1], axis=0, keepdims=True)

    hq = hq_ref[0]
    q = hq * jax.nn.sigmoid(hq)
    f = lb + (1.0 - lb) * jax.nn.sigmoid(hf_ref[0])
    k = 1.0 - f
    logf = jnp.log(f)

    r = lax.broadcasted_iota(jnp.int32, (lc, lc), 0)
    c = lax.broadcasted_iota(jnp.int32, (lc, lc), 1)
    same = (r // T) == (c // T)
    tri = jnp.where(same & (c <= r), 1.0, 0.0).astype(F32)
    ones = jnp.where(same, 1.0, 0.0).astype(F32)
    b = jnp.dot(tri, logf, precision=lax.Precision.HIGHEST, preferred_element_type=F32)
    bl = jnp.dot(ones, logf, precision=lax.Precision.HIGHEST, preferred_element_type=F32)

    q_s[...] = q
    k_s[...] = k
    b_s[...] = b
    v_s[...] = hi_ref[0]
    qd_s[...] = q * jnp.exp(b)
    kd_s[...] = k * jnp.exp(bl - b)
    ds_s[...] = jnp.exp(bl)

    row = lax.broadcasted_iota(jnp.int32, (T, HG_DK), 0)

    def sub(i, carry):
        r0 = pl.multiple_of(i * T, T)
        qi = q_s[pl.ds(r0, T), :]
        bi = b_s[pl.ds(r0, T), :]
        acc = jnp.zeros((T, HG_DV), F32)
        for s in range(T):
            ks = k_s[pl.ds(r0 + s, 1), :]
            bs = b_s[pl.ds(r0 + s, 1), :]
            vs = v_s[pl.ds(r0 + s, 1), :]
            decay = jnp.exp(jnp.where(row >= s, bi - bs, NEG))
            col = jnp.sum(qi * ks * decay, axis=-1, keepdims=True)
            acc = acc + col * vs
        st = st_ref[...]
        acc = acc + lax.dot_general(qd_s[pl.ds(r0, T), :], st, (((1,), (1,)), ((), ())),
                                    preferred_element_type=F32)
        o_ref[0, pl.ds(r0, T), :] = acc
        upd = lax.dot_general(v_s[pl.ds(r0, T), :], kd_s[pl.ds(r0, T), :], (((0,), (0,)), ((), ())),
                              preferred_element_type=F32)
        st_ref[...] = st * ds_s[pl.ds(r0, 1), :] + upd
        return carry

    lax.fori_loop(0, lc // T, sub, 0)

    hg = hg_ref[0]
    o_ref[0] = _rms(o_ref[0], gain_ref[...]) * (hg * jax.nn.sigmoid(hg))


def _hgrn(zh, lb_table, gain, layer, lc):
    B, S, _ = zh.shape
    H = HG_HEADS
    blk = lambda off: pl.BlockSpec((1, lc, HG_DK), lambda b, h, c: (b, c, off + h))
    n_lb = lb_table.shape[0]
    scr = lambda: pltpu.VMEM((lc, HG_DK), F32)
    return pl.pallas_call(
        functools.partial(_hgrn_body, lc=lc, layer=layer),
        grid=(B, H, S // lc),
        in_specs=[
            blk(0), blk(H), blk(2 * H), blk(3 * H),
            pl.BlockSpec((n_lb, HG_DK), lambda b, h, c: (0, h)),
            pl.BlockSpec((1, HG_DV), lambda b, h, c: (0, 0)),
        ],
        out_specs=pl.BlockSpec((1, lc, HG_DV), lambda b, h, c: (b, c, h)),
        out_shape=jax.ShapeDtypeStruct((B, S, H * HG_DV), F32),
        scratch_shapes=[pltpu.VMEM((HG_DV, HG_DK), F32)] + [scr() for _ in range(7)],
        compiler_params=_params("parallel", "parallel", "arbitrary"),
        name="hgrn2",
    )(zh, zh, zh, zh, lb_table, gain)


def _mix_out_body(a_ref, r_ref, x_ref, gm_ref, wa_ref, wr_ref, gp_ref, o_ref):
    an = _rms(a_ref[...], gm_ref[...]).astype(BF16)
    y = jnp.dot(an, wa_ref[...], preferred_element_type=F32)
    y = y + jnp.dot(r_ref[...].astype(BF16), wr_ref[...], preferred_element_type=F32)
    o_ref[...] = x_ref[...] + _rms(y, gp_ref[...])


def _mix_out(a2d, r2d, x2d, gm, wa, wr, gp, tm):
    N, D = x2d.shape
    return pl.pallas_call(
        _mix_out_body,
        grid=(N // tm,),
        in_specs=[
            pl.BlockSpec((tm, a2d.shape[1]), lambda i: (i, 0)),
            pl.BlockSpec((tm, r2d.shape[1]), lambda i: (i, 0)),
            pl.BlockSpec((tm, D), lambda i: (i, 0)),
            _const_spec(gm.shape), _const_spec(wa.shape), _const_spec(wr.shape), _const_spec(gp.shape),
        ],
        out_specs=pl.BlockSpec((tm, D), lambda i: (i, 0)),
        out_shape=jax.ShapeDtypeStruct((N, D), F32),
        compiler_params=_params("parallel"),
        name="mix_out",
    )(a2d, r2d, x2d, gm, wa, wr, gp)


def _xattn_body(h_ref, gpre_ref, wq_ref, kv_ref, wo_ref, gpost_ref, o_ref, oh_ref, *, q_scale):
    h = h_ref[...]
    D = h.shape[-1]
    dh = D // X_HEADS
    xn = _rms(h, gpre_ref[...]).astype(BF16)
    qx = (jnp.dot(xn, wq_ref[...], preferred_element_type=F32) * q_scale).astype(BF16)
    for hd in range(X_HEADS):
        qh = qx[:, hd * dh:(hd + 1) * dh]
        kh = kv_ref[0, :, hd * dh:(hd + 1) * dh]
        vh = kv_ref[0, :, D + hd * dh:D + (hd + 1) * dh]
        s = lax.dot_general(qh, kh, (((1,), (1,)), ((), ())), preferred_element_type=F32)
        p = jnp.exp2(s - jnp.max(s, axis=-1, keepdims=True))
        l = jnp.sum(p, axis=-1, keepdims=True)
        oh = jnp.dot(p.astype(BF16), vh, preferred_element_type=F32) / l
        oh_ref[:, hd * dh:(hd + 1) * dh] = oh.astype(BF16)
    ox = jnp.dot(oh_ref[...], wo_ref[...], preferred_element_type=F32)
    o_ref[...] = h + _rms(ox, gpost_ref[...])


def _xattn(h2d, gpre, wq, kvm, wo, gpost, tm, tiles_per_batch):
    N, D = h2d.shape
    M = kvm.shape[1]
    q_scale = ((D // X_HEADS) ** -0.5) * LOG2E
    return pl.pallas_call(
        functools.partial(_xattn_body, q_scale=q_scale),
        grid=(N // tm,),
        in_specs=[
            pl.BlockSpec((tm, D), lambda i: (i, 0)),
            _const_spec(gpre.shape), _const_spec(wq.shape),
            pl.BlockSpec((1, M, 2 * D), lambda i: (i // tiles_per_batch, 0, 0)),
            _const_spec(wo.shape), _const_spec(gpost.shape),
        ],
        out_specs=pl.BlockSpec((tm, D), lambda i: (i, 0)),
        out_shape=jax.ShapeDtypeStruct((N, D), F32),
        scratch_shapes=[pltpu.VMEM((tm, D), BF16)],
        compiler_params=_params("parallel"),
        name="xattn",
    )(h2d, gpre, wq, kvm, wo, gpost)


def _gelu_tanh(x):
    return x * (0.5 * (1.0 + jnp.tanh(math.sqrt(2.0 / math.pi) * (x + 0.044715 * (x * x * x)))))


def _ffn_body(h_ref, gpre_ref, wg_ref, wv_ref, cwg_ref, cwv_ref, cbg_ref, cbv_ref, wd_ref, gpost_ref,
              o_ref, xn_ref, cg_ref, cv_ref, buf_ref, *, tm, tiles_per_batch):
    i = pl.program_id(0)
    j = pl.program_id(1)
    nj = pl.num_programs(1)

    @pl.when(j == 0)
    def _():
        xn_ref[...] = _rms(h_ref[...], gpre_ref[...]).astype(BF16)

    first = (i % tiles_per_batch) == 0

    def conv_branch(w_ref, cw_ref, cb_ref, carry_ref):
        u = jnp.dot(xn_ref[...], w_ref[...], preferred_element_type=F32)
        @pl.when(first)
        def _():
            buf_ref[0:CONV_PAD, :] = jnp.zeros((CONV_PAD, u.shape[1]), F32)

        @pl.when(jnp.logical_not(first))
        def _():
            buf_ref[0:CONV_PAD, :] = carry_ref[j]

        buf_ref[CONV_PAD:CONV_PAD + tm, :] = u
        carry_ref[j] = u[tm - CONV_PAD:, :]
        cw = cw_ref[...]
        return (cw[0:1, :] * buf_ref[CONV_PAD - 2:CONV_PAD - 2 + tm, :]
                + cw[1:2, :] * buf_ref[CONV_PAD - 1:CONV_PAD - 1 + tm, :]
                + cw[2:3, :] * u + cb_ref[...])

    gate = conv_branch(wg_ref, cwg_ref, cbg_ref, cg_ref)
    val = conv_branch(wv_ref, cwv_ref, cbv_ref, cv_ref)
    act = (_gelu_tanh(gate) * val).astype(BF16)
    contrib = jnp.dot(act, wd_ref[...], preferred_element_type=F32)

    @pl.when(j == 0)
    def _():
        o_ref[...] = contrib

    @pl.when(j > 0)
    def _():
        o_ref[...] += contrib

    @pl.when(j == nj - 1)
    def _():
        o_ref[...] = h_ref[...] + _rms(o_ref[...], gpost_ref[...])


def _ffn(h2d, gpre, wg, wv, cwg, cwv, cbg, cbv, wd, gpost, tm, tf, tiles_per_batch):
    N, D = h2d.shape
    F = wg.shape[1]
    nj = F // tf
    col = lambda r: pl.BlockSpec((r, tf), lambda i, j: (0, j))
    return pl.pallas_call(
        functools.partial(_ffn_body, tm=tm, tiles_per_batch=tiles_per_batch),
        grid=(N // tm, nj),
        in_specs=[
            pl.BlockSpec((tm, D), lambda i, j: (i, 0)),
            _const_spec(gpre.shape),
            pl.BlockSpec((D, tf), lambda i, j: (0, j)),
            pl.BlockSpec((D, tf), lambda i, j: (0, j)),
            col(CONV_WIDTH), col(CONV_WIDTH), col(1), col(1),
            pl.BlockSpec((tf, D), lambda i, j: (j, 0)),
            _const_spec(gpost.shape),
        ],
        out_specs=pl.BlockSpec((tm, D), lambda i, j: (i, 0)),
        out_shape=jax.ShapeDtypeStruct((N, D), F32),
        scratch_shapes=[
            pltpu.VMEM((tm, D), BF16),
            pltpu.VMEM((nj, CONV_PAD, tf), F32),
            pltpu.VMEM((nj, CONV_PAD, tf), F32),
            pltpu.VMEM((CONV_PAD + tm, tf), F32),
        ],
        compiler_params=_params("arbitrary", "arbitrary"),
        name="conv_ffn",
    )(h2d, gpre, wg, wv, cwg, cwv, cbg, cbv, wd, gpost)


def kernel(x, mem, w_in, q_norm, w_uq, kv_norm, w_ukv, mla_out_norm, hgrn_lb, hgrn_out_norm, w_out,
           ln_mix_pre, ln_mix_post, ln_x_pre, ln_x_post, mem_norm, w_xq, w_xk, w_xv, w_xo,
           ln_ffn_pre, ln_ffn_post, w_up, conv_w, conv_b, w_down):
    B, S, D = x.shape
    M = mem.shape[1]
    depth = w_in.shape[0]
    N = B * S
    tm = min(ROW_TILE, S)
    blk = min(ATT_BLK, S)
    lc = min(HG_BLK, S)
    assert S % tm == 0 and S % blk == 0 and S % lc == 0 and blk % CHUNK == 0 and lc % HG_SUB == 0
    tiles_per_batch = S // tm
    d_ff = w_down.shape[1]
    tf = FF_TILE
    assert d_ff % tf == 0
    n_mla_in = MLA_Q_RANK + MLA_KV_RANK + MLA_ROPE

    pos = jnp.arange(S, dtype=F32)
    inv_freq = 1.0 / (ROPE_THETA ** (jnp.arange(0, MLA_ROPE, 2, dtype=F32) / MLA_ROPE))
    ang = pos[:, None] * inv_freq[None, :]
    cos, sin = jnp.cos(ang), jnp.sin(ang)
    cosT, sinT = cos.T, sin.T

    row = lambda v: v.reshape(1, -1).astype(F32)
    h = x
    for l in range(depth):
        w_mla = w_in[l][:, :n_mla_in].astype(BF16)
        w_hg = w_in[l][:, n_mla_in:].astype(BF16)
        w_uqT = w_uq[l].T.astype(BF16)
        w_ukv_h = w_ukv[l].reshape(MLA_KV_RANK, MLA_HEADS, MLA_NOPE + MLA_V)
        w_uk = w_ukv_h[:, :, :MLA_NOPE].reshape(MLA_KV_RANK, MLA_HEADS * MLA_NOPE).astype(BF16)
        w_uvT = w_ukv_h[:, :, MLA_NOPE:].reshape(MLA_KV_RANK, MLA_HEADS * MLA_V).T.astype(BF16)
        w_out_a = w_out[l][:MLA_WIDTH].astype(BF16)
        w_out_r = w_out[l][MLA_WIDTH:].astype(BF16)
        w_xkv = jnp.concatenate([w_xk[l], w_xv[l]], axis=1).astype(BF16)
        w_up_g = w_up[l][:, :d_ff].astype(BF16)
        w_up_v = w_up[l][:, d_ff:].astype(BF16)

        qT, k, vT = _mla_prep(h, row(ln_mix_pre[l]), w_mla, row(q_norm[l]), w_uqT, row(kv_norm[l]),
                              w_uk, w_uvT, cos, sin, cosT, sinT, blk)
        a = _attention(qT, k, vT, blk)
        zh = _norm_matmul(h.reshape(N, D), row(ln_mix_pre[l]), w_hg, F32, tm, 1024, "hgrn_in")
        r = _hgrn(zh.reshape(B, S, -1), hgrn_lb.astype(F32), row(hgrn_out_norm[l]), l, lc)
        h2d = _mix_out(a.reshape(N, -1), r.reshape(N, -1), h.reshape(N, D), row(mla_out_norm[l]),
                       w_out_a, w_out_r, row(ln_mix_post[l]), tm)

        kvm = _norm_matmul(mem.reshape(B * M, D), row(mem_norm[l]), w_xkv, BF16, min(256, B * M), 1024, "mem_kv")
        h2d = _xattn(h2d, row(ln_x_pre[l]), w_xq[l].astype(BF16), kvm.reshape(B, M, 2 * D),
                     w_xo[l].astype(BF16), row(ln_x_post[l]), tm, tiles_per_batch)

        h2d = _ffn(h2d, row(ln_ffn_pre[l]), w_up_g, w_up_v, conv_w[l][:, :d_ff], conv_w[l][:, d_ff:],
                   row(conv_b[l][:d_ff]), row(conv_b[l][d_ff:]), w_down[l].astype(BF16),
                   row(ln_ffn_post[l]), tm, tf, tiles_per_batch)
        h = h2d.reshape(B, S, D)
    return h
```

```python
import functools
import math

import jax
import jax.numpy as jnp
from jax import lax
from jax.experimental import pallas as pl
from jax.experimental.pallas import tpu as pltpu

F32 = jnp.float32
BF16 = jnp.bfloat16

EPS = 1e-6
CHUNK = 64
MLA_V = 128
MLA_NOPE = 128
MLA_ROPE = 64
MLA_HEADS = 8
MLA_QK = MLA_NOPE + MLA_ROPE
MLA_Q_RANK = 512
MLA_KV_RANK = 256
ROPE_THETA = 10000.0
HG_DK = 128
HG_DV = 128
HG_HEADS = 8
HG_WIDTH = HG_HEADS * HG_DV
MLA_WIDTH = MLA_HEADS * MLA_V
X_HEADS = 4
CONV_WIDTH = 3

LOG2E = 1.4426950408889634
NEG = -1e30

VMEM_LIMIT_BYTES = 56 * 1024 * 1024

ROW_TILE = 512
ATT_BLK = 512
ATT_QBLK = 2048
HG_BLK = 256
HG_SUB = 16
FF_TILE = 512
CONV_PAD = 8


def _params(*sem):
    return pltpu.CompilerParams(dimension_semantics=sem, vmem_limit_bytes=VMEM_LIMIT_BYTES)


def _rms(xf, g):
    return xf * lax.rsqrt(jnp.mean(xf * xf, axis=-1, keepdims=True) + EPS) * g


def _const_spec(shape):
    nd = len(shape)
    return pl.BlockSpec(shape, lambda *_: (0,) * nd, pipeline_mode=pl.Buffered(1))


def _mla_prep_body(x_ref, g_ref, wm_ref, qn_ref, wuqT_ref, kvn_ref, wuk_ref, wuvT_ref,
                   cos_ref, sin_ref, cosT_ref, sinT_ref, qT_ref, k_ref, vT_ref, *, q_scale):
    xn = _rms(x_ref[0], g_ref[...]).astype(BF16)
    z = jnp.dot(xn, wm_ref[...], preferred_element_type=F32)
    cq = _rms(z[:, :MLA_Q_RANK], qn_ref[...]).astype(BF16)
    ckv = _rms(z[:, MLA_Q_RANK:MLA_Q_RANK + MLA_KV_RANK], kvn_ref[...]).astype(BF16)
    kr = z[:, MLA_Q_RANK + MLA_KV_RANK:]

    nt = (((1,), (1,)), ((), ()))
    qT = lax.dot_general(wuqT_ref[...], cq, nt, preferred_element_type=F32)
    cosT = cosT_ref[...]
    sinT = sinT_ref[...]
    half = MLA_ROPE // 2
    for h in range(MLA_HEADS):
        base = h * MLA_QK
        x1 = qT[base + MLA_NOPE:base + MLA_NOPE + half]
        x2 = qT[base + MLA_NOPE + half:base + MLA_QK]
        qT_ref[0, base:base + MLA_NOPE, :] = (qT[base:base + MLA_NOPE] * q_scale).astype(BF16)
        qT_ref[0, base + MLA_NOPE:base + MLA_NOPE + half, :] = ((x1 * cosT - x2 * sinT) * q_scale).astype(BF16)
        qT_ref[0, base + MLA_NOPE + half:base + MLA_QK, :] = ((x2 * cosT + x1 * sinT) * q_scale).astype(BF16)

    kn = jnp.dot(ckv, wuk_ref[...], preferred_element_type=F32)
    cos = cos_ref[...]
    sin = sin_ref[...]
    k1 = kr[:, :half]
    k2 = kr[:, half:]
    krope = jnp.concatenate([k1 * cos - k2 * sin, k2 * cos + k1 * sin], axis=-1).astype(BF16)
    for h in range(MLA_HEADS):
        k_ref[0, h, :, 0:MLA_NOPE] = kn[:, h * MLA_NOPE:(h + 1) * MLA_NOPE].astype(BF16)
        k_ref[0, h, :, MLA_NOPE:MLA_QK] = krope

    vT = lax.dot_general(wuvT_ref[...], ckv, nt, preferred_element_type=F32)
    for h in range(MLA_HEADS):
        vT_ref[0, h, 0] = vT[h * MLA_V:(h + 1) * MLA_V].astype(BF16)


def _mla_prep(x, g, wm, qn, wuqT, kvn, wuk, wuvT, cos, sin, cosT, sinT, blk):
    B, S, D = x.shape
    nb = S // blk
    half = MLA_ROPE // 2
    q_scale = (MLA_QK ** -0.5) * LOG2E
    return pl.pallas_call(
        functools.partial(_mla_prep_body, q_scale=q_scale),
        grid=(B, nb),
        in_specs=[
            pl.BlockSpec((1, blk, D), lambda b, i: (b, i, 0)),
            _const_spec(g.shape), _const_spec(wm.shape), _const_spec(qn.shape), _const_spec(wuqT.shape),
            _const_spec(kvn.shape), _const_spec(wuk.shape), _const_spec(wuvT.shape),
            pl.BlockSpec((blk, half), lambda b, i: (i, 0)),
            pl.BlockSpec((blk, half), lambda b, i: (i, 0)),
            pl.BlockSpec((half, blk), lambda b, i: (0, i)),
            pl.BlockSpec((half, blk), lambda b, i: (0, i)),
        ],
        out_specs=[
            pl.BlockSpec((1, MLA_HEADS * MLA_QK, blk), lambda b, i: (b, 0, i)),
            pl.BlockSpec((1, MLA_HEADS, blk, MLA_QK), lambda b, i: (b, 0, i, 0)),
            pl.BlockSpec((1, MLA_HEADS, 1, MLA_V, blk), lambda b, i: (b, 0, i, 0, 0)),
        ],
        out_shape=[
            jax.ShapeDtypeStruct((B, MLA_HEADS * MLA_QK, S), BF16),
            jax.ShapeDtypeStruct((B, MLA_HEADS, S, MLA_QK), BF16),
            jax.ShapeDtypeStruct((B, MLA_HEADS, nb, MLA_V, blk), BF16),
        ],
        compiler_params=_params("parallel", "parallel"),
        name="mla_prep",
    )(x, g, wm, qn, wuqT, kvn, wuk, wuvT, cos, sin, cosT, sinT)


def _attn_body(qT_ref, k_ref, vT_ref, o_ref, m_ref, l_ref, acc_ref, *, bq, bk):
    i = pl.program_id(2)
    ratio = bq // bk
    qT = qT_ref[0]
    m_ref[...] = jnp.full(m_ref.shape, NEG, F32)
    l_ref[...] = jnp.zeros(l_ref.shape, F32)
    acc_ref[...] = jnp.zeros(acc_ref.shape, F32)

    def update(j, c0, masked):
        kblk = k_ref[0, 0, pl.ds(pl.multiple_of(j * bk, bk), bk), :]
        s = jnp.dot(kblk, qT[:, c0:], preferred_element_type=F32)
        if masked:
            key_chunk = lax.broadcasted_iota(jnp.int32, s.shape, 0) // CHUNK
            qry_chunk = lax.broadcasted_iota(jnp.int32, s.shape, 1) // CHUNK
            s = jnp.where(key_chunk <= qry_chunk, s, NEG)
        m_prev = m_ref[:, c0:]
        m_new = jnp.maximum(m_prev, jnp.max(s, axis=0, keepdims=True))
        alpha = jnp.exp2(m_prev - m_new)
        p = jnp.exp2(s - m_new)
        l_ref[:, c0:] = alpha * l_ref[:, c0:] + jnp.sum(p, axis=0, keepdims=True)
        acc_ref[:, c0:] = alpha * acc_ref[:, c0:] + jnp.dot(vT_ref[0, 0, j], p.astype(BF16),
                                                            preferred_element_type=F32)
        m_ref[:, c0:] = m_new

    def body(j, carry):
        update(j, 0, False)
        return carry

    lax.fori_loop(0, i * ratio, body, 0)
    for d in range(ratio):
        update(i * ratio + d, d * bk, True)
    o_ref[0] = (acc_ref[...] / l_ref[...]).T


def _attention(qT, k, vT, bq, bk):
    B, H, S, _ = k.shape
    return pl.pallas_call(
        functools.partial(_attn_body, bq=bq, bk=bk),
        grid=(B, H, S // bq),
        in_specs=[
            pl.BlockSpec((1, MLA_QK, bq), lambda b, h, i: (b, h, i)),
            pl.BlockSpec((1, 1, S, MLA_QK), lambda b, h, i: (b, h, 0, 0)),
            pl.BlockSpec((1, 1, S // bk, MLA_V, bk), lambda b, h, i: (b, h, 0, 0, 0)),
        ],
        out_specs=pl.BlockSpec((1, bq, MLA_V), lambda b, h, i: (b, i, h)),
        out_shape=jax.ShapeDtypeStruct((B, S, H * MLA_V), F32),
        scratch_shapes=[pltpu.VMEM((1, bq), F32), pltpu.VMEM((1, bq), F32), pltpu.VMEM((MLA_V, bq), F32)],
        compiler_params=_params("parallel", "parallel", "arbitrary"),
        name="mla_attention",
    )(qT, k, vT)


def _norm_matmul_body(x_ref, g_ref, w_ref, o_ref, xn_ref):
    @pl.when(pl.program_id(1) == 0)
    def _():
        xn_ref[...] = _rms(x_ref[...], g_ref[...]).astype(BF16)

    o_ref[...] = jnp.dot(xn_ref[...], w_ref[...], preferred_element_type=F32).astype(o_ref.dtype)


def _norm_matmul(x2d, g, w, out_dtype, tm, tn, name):
    N, D = x2d.shape
    M = w.shape[1]
    return pl.pallas_call(
        _norm_matmul_body,
        grid=(N // tm, M // tn),
        in_specs=[
            pl.BlockSpec((tm, D), lambda i, j: (i, 0)),
            _const_spec(g.shape),
            pl.BlockSpec((D, tn), lambda i, j: (0, j)),
        ],
        out_specs=pl.BlockSpec((tm, tn), lambda i, j: (i, j)),
        out_shape=jax.ShapeDtypeStruct((N, M), out_dtype),
        scratch_shapes=[pltpu.VMEM((tm, D), BF16)],
        compiler_params=_params("parallel", "arbitrary"),
        name=name,
    )(x2d, g, w)


def _group_scan(x, row_in_group, group, reverse):
    n = x.shape[0]
    sh = 1
    while sh < group:
        if reverse:
            x = x + jnp.where(row_in_group < group - sh, pltpu.roll(x, n - sh, axis=0), 0.0)
        else:
            x = x + jnp.where(row_in_group >= sh, pltpu.roll(x, sh, axis=0), 0.0)
        sh *= 2
    return x


def _hgrn_body(hq_ref, hf_ref, hi_ref, hg_ref, lbp_ref, gain_ref, o_ref,
               st_ref, q_s, k_s, b_s, v_s, qd_s, kd_s, es_s, o_s, *, lc, layer):
    T = HG_SUB
    H = HG_HEADS
    half = T // 2

    @pl.when(pl.program_id(1) == 0)
    def _():
        st_ref[...] = jnp.zeros(st_ref.shape, F32)

    lbp = lbp_ref[...]
    e = jnp.exp(lbp - jnp.max(lbp, axis=0, keepdims=True))
    prob = e / jnp.sum(e, axis=0, keepdims=True)
    lb = jnp.sum(prob[:layer + 1], axis=0, keepdims=True)

    hq = hq_ref[0]
    q = hq * jax.nn.sigmoid(hq)
    f = lb + (1.0 - lb) * jax.nn.sigmoid(hf_ref[0])
    k = 1.0 - f
    logf = jnp.log(f)

    rig = lax.broadcasted_iota(jnp.int32, logf.shape, 0) % T
    b = _group_scan(logf, rig, T, reverse=False)
    suffix = _group_scan(logf, rig, T, reverse=True)
    staged = ((q_s, q), (k_s, k), (b_s, b), (v_s, hi_ref[0]), (qd_s, q * jnp.exp(b)),
              (kd_s, k * jnp.exp(suffix - logf)),
              (es_s, jnp.exp(suffix)))
    for ref, val in staged:
        for h in range(H):
            ref[h] = val[:, h * HG_DK:(h + 1) * HG_DK]

    row = lax.broadcasted_iota(jnp.int32, (half, HG_DK), 0)
    nt = (((1,), (1,)), ((), ()))
    tn = (((0,), (0,)), ((), ()))

    def sub(i, carry):
        r0 = pl.multiple_of(i * T, T)
        r1 = pl.multiple_of(i * T + half, half)
        for h in range(H):
            cs = slice(h * HG_DK, (h + 1) * HG_DK)
            q_lo = q_s[h, pl.ds(r0, half), :]
            q_hi = q_s[h, pl.ds(r1, half), :]
            b_lo = b_s[h, pl.ds(r0, half), :]
            b_hi = b_s[h, pl.ds(r1, half), :]
            acc_lo = jnp.zeros((half, HG_DV), F32)
            acc_hi = jnp.zeros((half, HG_DV), F32)
            for s in range(T):
                ks = k_s[h, pl.ds(r0 + s, 1), :]
                bs = b_s[h, pl.ds(r0 + s, 1), :]
                vs = v_s[h, pl.ds(r0 + s, 1), :]
                if s < half:
                    d_lo = jnp.exp(jnp.where(row >= s, b_lo - bs, NEG))
                    acc_lo = acc_lo + jnp.sum(q_lo * ks * d_lo, axis=-1, keepdims=True) * vs
                    d_hi = jnp.exp(b_hi - bs)
                else:
                    d_hi = jnp.exp(jnp.where(row >= s - half, b_hi - bs, NEG))
                acc_hi = acc_hi + jnp.sum(q_hi * ks * d_hi, axis=-1, keepdims=True) * vs
            st = st_ref[h]
            inter = lax.dot_general(qd_s[h, pl.ds(r0, T), :], st, nt, preferred_element_type=F32)
            o_s[h, pl.ds(r0, half), :] = acc_lo + inter[:half]
            o_s[h, pl.ds(r1, half), :] = acc_hi + inter[half:]
            upd = lax.dot_general(v_s[h, pl.ds(r0, T), :], kd_s[h, pl.ds(r0, T), :], tn,
                                  preferred_element_type=F32)
            st_ref[h] = st * es_s[h, pl.ds(r0, 1), :] + upd
        return carry

    lax.fori_loop(0, lc // T, sub, 0)

    gain = gain_ref[...]
    for h in range(H):
        cs = slice(h * HG_DV, (h + 1) * HG_DV)
        hg = hg_ref[0, :, cs]
        o_ref[0, :, cs] = _rms(o_s[h], gain) * (hg * jax.nn.sigmoid(hg))


def _hgrn(zh, lb_table, gain, layer, lc):
    B, S, _ = zh.shape
    W = HG_HEADS * HG_DK
    blk = lambda part: pl.BlockSpec((1, lc, W), lambda b, c: (b, c, part))
    scr = lambda: pltpu.VMEM((HG_HEADS, lc, HG_DK), F32)
    return pl.pallas_call(
        functools.partial(_hgrn_body, lc=lc, layer=layer),
        grid=(B, S // lc),
        in_specs=[
            blk(0), blk(1), blk(2), blk(3),
            pl.BlockSpec(lb_table.shape, lambda b, c: (0, 0)),
            pl.BlockSpec((1, HG_DV), lambda b, c: (0, 0)),
        ],
        out_specs=pl.BlockSpec((1, lc, W), lambda b, c: (b, c, 0)),
        out_shape=jax.ShapeDtypeStruct((B, S, W), F32),
        scratch_shapes=[pltpu.VMEM((HG_HEADS, HG_DV, HG_DK), F32)] + [scr() for _ in range(8)],
        compiler_params=_params("parallel", "arbitrary"),
        name="hgrn2",
    )(zh, zh, zh, zh, lb_table, gain)


def _mix_out_body(a_ref, r_ref, x_ref, gm_ref, wa_ref, wr_ref, gp_ref, o_ref):
    an = _rms(a_ref[...], gm_ref[...]).astype(BF16)
    y = jnp.dot(an, wa_ref[...], preferred_element_type=F32)
    y = y + jnp.dot(r_ref[...].astype(BF16), wr_ref[...], preferred_element_type=F32)
    o_ref[...] = x_ref[...] + _rms(y, gp_ref[...])


def _mix_out(a2d, r2d, x2d, gm, wa, wr, gp, tm):
    N, D = x2d.shape
    return pl.pallas_call(
        _mix_out_body,
        grid=(N // tm,),
        in_specs=[
            pl.BlockSpec((tm, a2d.shape[1]), lambda i: (i, 0)),
            pl.BlockSpec((tm, r2d.shape[1]), lambda i: (i, 0)),
            pl.BlockSpec((tm, D), lambda i: (i, 0)),
            _const_spec(gm.shape), _const_spec(wa.shape), _const_spec(wr.shape), _const_spec(gp.shape),
        ],
        out_specs=pl.BlockSpec((tm, D), lambda i: (i, 0)),
        out_shape=jax.ShapeDtypeStruct((N, D), F32),
        compiler_params=_params("parallel"),
        name="mix_out",
    )(a2d, r2d, x2d, gm, wa, wr, gp)


def _xattn_body(h_ref, gpre_ref, wq_ref, kv_ref, wo_ref, gpost_ref, o_ref, oh_ref, *, q_scale):
    h = h_ref[...]
    D = h.shape[-1]
    dh = D // X_HEADS
    xn = _rms(h, gpre_ref[...]).astype(BF16)
    qx = (jnp.dot(xn, wq_ref[...], preferred_element_type=F32) * q_scale).astype(BF16)
    for hd in range(X_HEADS):
        qh = qx[:, hd * dh:(hd + 1) * dh]
        kh = kv_ref[0, :, hd * dh:(hd + 1) * dh]
        vh = kv_ref[0, :, D + hd * dh:D + (hd + 1) * dh]
        s = lax.dot_general(qh, kh, (((1,), (1,)), ((), ())), preferred_element_type=F32)
        p = jnp.exp2(s - jnp.max(s, axis=-1, keepdims=True))
        l = jnp.sum(p, axis=-1, keepdims=True)
        oh = jnp.dot(p.astype(BF16), vh, preferred_element_type=F32) / l
        oh_ref[:, hd * dh:(hd + 1) * dh] = oh.astype(BF16)
    ox = jnp.dot(oh_ref[...], wo_ref[...], preferred_element_type=F32)
    o_ref[...] = h + _rms(ox, gpost_ref[...])


def _xattn(h2d, gpre, wq, kvm, wo, gpost, tm, tiles_per_batch):
    N, D = h2d.shape
    M = kvm.shape[1]
    q_scale = ((D // X_HEADS) ** -0.5) * LOG2E
    return pl.pallas_call(
        functools.partial(_xattn_body, q_scale=q_scale),
        grid=(N // tm,),
        in_specs=[
            pl.BlockSpec((tm, D), lambda i: (i, 0)),
            _const_spec(gpre.shape), _const_spec(wq.shape),
            pl.BlockSpec((1, M, 2 * D), lambda i: (i // tiles_per_batch, 0, 0)),
            _const_spec(wo.shape), _const_spec(gpost.shape),
        ],
        out_specs=pl.BlockSpec((tm, D), lambda i: (i, 0)),
        out_shape=jax.ShapeDtypeStruct((N, D), F32),
        scratch_shapes=[pltpu.VMEM((tm, D), BF16)],
        compiler_params=_params("parallel"),
        name="xattn",
    )(h2d, gpre, wq, kvm, wo, gpost)


def _gelu_tanh(x):
    return x * (0.5 * (1.0 + jnp.tanh(math.sqrt(2.0 / math.pi) * (x + 0.044715 * (x * x * x)))))


def _ffn_body(h_ref, gpre_ref, wg_ref, wv_ref, cwg_ref, cwv_ref, cbg_ref, cbv_ref, wd_ref, gpost_ref,
              o_ref, xn_ref, cg_ref, cv_ref, buf_ref, *, tm, tiles_per_batch):
    i = pl.program_id(0)
    j = pl.program_id(1)
    nj = pl.num_programs(1)

    @pl.when(j == 0)
    def _():
        xn_ref[...] = _rms(h_ref[...], gpre_ref[...]).astype(BF16)

    first = (i % tiles_per_batch) == 0

    def conv_branch(w_ref, cw_ref, cb_ref, carry_ref):
        u = jnp.dot(xn_ref[...], w_ref[...], preferred_element_type=F32)

        @pl.when(first)
        def _():
            buf_ref[0:CONV_PAD, :] = jnp.zeros((CONV_PAD, u.shape[1]), F32)

        @pl.when(jnp.logical_not(first))
        def _():
            buf_ref[0:CONV_PAD, :] = carry_ref[j]

        buf_ref[CONV_PAD:CONV_PAD + tm, :] = u
        carry_ref[j] = u[tm - CONV_PAD:, :]
        cw = cw_ref[...]
        return (cw[0:1, :] * buf_ref[CONV_PAD - 2:CONV_PAD - 2 + tm, :]
                + cw[1:2, :] * buf_ref[CONV_PAD - 1:CONV_PAD - 1 + tm, :]
                + cw[2:3, :] * u + cb_ref[...])

    gate = conv_branch(wg_ref, cwg_ref, cbg_ref, cg_ref)
    val = conv_branch(wv_ref, cwv_ref, cbv_ref, cv_ref)
    act = (_gelu_tanh(gate) * val).astype(BF16)
    contrib = jnp.dot(act, wd_ref[...], preferred_element_type=F32)

    @pl.when(j == 0)
    def _():
        o_ref[...] = contrib

    @pl.when(j > 0)
    def _():
        o_ref[...] += contrib

    @pl.when(j == nj - 1)
    def _():
        o_ref[...] = h_ref[...] + _rms(o_ref[...], gpost_ref[...])


def _ffn(h2d, gpre, wg, wv, cwg, cwv, cbg, cbv, wd, gpost, tm, tf, tiles_per_batch):
    N, D = h2d.shape
    F = wg.shape[1]
    nj = F // tf
    col = lambda r: pl.BlockSpec((r, tf), lambda i, j: (0, j))
    return pl.pallas_call(
        functools.partial(_ffn_body, tm=tm, tiles_per_batch=tiles_per_batch),
        grid=(N // tm, nj),
        in_specs=[
            pl.BlockSpec((tm, D), lambda i, j: (i, 0)),
            _const_spec(gpre.shape),
            pl.BlockSpec((D, tf), lambda i, j: (0, j)),
            pl.BlockSpec((D, tf), lambda i, j: (0, j)),
            col(CONV_WIDTH), col(CONV_WIDTH), col(1), col(1),
            pl.BlockSpec((tf, D), lambda i, j: (j, 0)),
            _const_spec(gpost.shape),
        ],
        out_specs=pl.BlockSpec((tm, D), lambda i, j: (i, 0)),
        out_shape=jax.ShapeDtypeStruct((N, D), F32),
        scratch_shapes=[
            pltpu.VMEM((tm, D), BF16),
            pltpu.VMEM((nj, CONV_PAD, tf), F32),
            pltpu.VMEM((nj, CONV_PAD, tf), F32),
            pltpu.VMEM((CONV_PAD + tm, tf), F32),
        ],
        compiler_params=_params("arbitrary", "arbitrary"),
        name="conv_ffn",
    )(h2d, gpre, wg, wv, cwg, cwv, cbg, cbv, wd, gpost)


def kernel(x, mem, w_in, q_norm, w_uq, kv_norm, w_ukv, mla_out_norm, hgrn_lb, hgrn_out_norm, w_out,
           ln_mix_pre, ln_mix_post, ln_x_pre, ln_x_post, mem_norm, w_xq, w_xk, w_xv, w_xo,
           ln_ffn_pre, ln_ffn_post, w_up, conv_w, conv_b, w_down):
    B, S, D = x.shape
    M = mem.shape[1]
    depth = w_in.shape[0]
    N = B * S
    tm = min(ROW_TILE, S)
    blk = min(ATT_BLK, S)
    bq = min(ATT_QBLK, S)
    lc = min(HG_BLK, S)
    assert S % tm == 0 and S % blk == 0 and S % bq == 0 and bq % blk == 0 and blk % CHUNK == 0
    assert S % lc == 0 and lc % HG_SUB == 0
    tiles_per_batch = S // tm
    d_ff = w_down.shape[1]
    tf = FF_TILE
    assert d_ff % tf == 0
    n_mla_in = MLA_Q_RANK + MLA_KV_RANK + MLA_ROPE

    pos = jnp.arange(S, dtype=F32)
    inv_freq = 1.0 / (ROPE_THETA ** (jnp.arange(0, MLA_ROPE, 2, dtype=F32) / MLA_ROPE))
    ang = pos[:, None] * inv_freq[None, :]
    cos, sin = jnp.cos(ang), jnp.sin(ang)
    cosT, sinT = cos.T, sin.T

    row = lambda v: v.reshape(1, -1).astype(F32)
    h = x
    for l in range(depth):
        w_mla = w_in[l][:, :n_mla_in].astype(BF16)
        w_hg = w_in[l][:, n_mla_in:].astype(BF16)
        w_uqT = w_uq[l].T.astype(BF16)
        w_ukv_h = w_ukv[l].reshape(MLA_KV_RANK, MLA_HEADS, MLA_NOPE + MLA_V)
        w_uk = w_ukv_h[:, :, :MLA_NOPE].reshape(MLA_KV_RANK, MLA_HEADS * MLA_NOPE).astype(BF16)
        w_uvT = w_ukv_h[:, :, MLA_NOPE:].reshape(MLA_KV_RANK, MLA_HEADS * MLA_V).T.astype(BF16)
        w_out_a = w_out[l][:MLA_WIDTH].astype(BF16)
        w_out_r = w_out[l][MLA_WIDTH:].astype(BF16)
        w_xkv = jnp.concatenate([w_xk[l], w_xv[l]], axis=1).astype(BF16)
        w_up_g = w_up[l][:, :d_ff].astype(BF16)
        w_up_v = w_up[l][:, d_ff:].astype(BF16)

        qT, k, vT = _mla_prep(h, row(ln_mix_pre[l]), w_mla, row(q_norm[l]), w_uqT, row(kv_norm[l]),
                              w_uk, w_uvT, cos, sin, cosT, sinT, blk)
        a = _attention(qT, k, vT, bq, blk)
        zh = _norm_matmul(h.reshape(N, D), row(ln_mix_pre[l]), w_hg, F32, tm, 1024, "hgrn_in")
        r = _hgrn(zh.reshape(B, S, -1), hgrn_lb.astype(F32), row(hgrn_out_norm[l]), l, lc)
        h2d = _mix_out(a.reshape(N, -1), r.reshape(N, -1), h.reshape(N, D), row(mla_out_norm[l]),
                       w_out_a, w_out_r, row(ln_mix_post[l]), tm)

        kvm = _norm_matmul(mem.reshape(B * M, D), row(mem_norm[l]), w_xkv, BF16, min(256, B * M), 1024, "mem_kv")
        h2d = _xattn(h2d, row(ln_x_pre[l]), w_xq[l].astype(BF16), kvm.reshape(B, M, 2 * D),
                     w_xo[l].astype(BF16), row(ln_x_post[l]), tm, tiles_per_batch)

        h2d = _ffn(h2d, row(ln_ffn_pre[l]), w_up_g, w_up_v, conv_w[l][:, :d_ff], conv_w[l][:, d_ff:],
                   row(conv_b[l][:d_ff]), row(conv_b[l][d_ff:]), w_down[l].astype(BF16),
                   row(ln_ffn_post[l]), tm, tf, tiles_per_batch)
        h = h2d.reshape(B, S, D)
    return h
```

```python
import functools
import math

import jax
import jax.numpy as jnp
from jax import lax
from jax.experimental import pallas as pl
from jax.experimental.pallas import tpu as pltpu

F32 = jnp.float32
BF16 = jnp.bfloat16

EPS = 1e-6
CHUNK = 64
MLA_V = 128
MLA_NOPE = 128
MLA_ROPE = 64
MLA_HEADS = 8
MLA_QK = MLA_NOPE + MLA_ROPE
MLA_Q_RANK = 512
MLA_KV_RANK = 256
ROPE_THETA = 10000.0
HG_DK = 128
HG_DV = 128
HG_HEADS = 8
HG_WIDTH = HG_HEADS * HG_DV
MLA_WIDTH = MLA_HEADS * MLA_V
X_HEADS = 4
CONV_WIDTH = 3

LOG2E = 1.4426950408889634
NEG = -1e30

VMEM_LIMIT_BYTES = 56 * 1024 * 1024

ROW_TILE = 512
ATT_BLK = 512
ATT_QBLK = 2048
HG_BLK = 256
HG_SUB = 16
FF_TILE = 512
FF_ROWS = 1024
FF_CHUNK = 64
FF_DOWN_ROWS = 256
CONV_PAD = 8


def _params(*sem):
    return pltpu.CompilerParams(dimension_semantics=sem, vmem_limit_bytes=VMEM_LIMIT_BYTES)


def _rms(xf, g):
    return xf * lax.rsqrt(jnp.mean(xf * xf, axis=-1, keepdims=True) + EPS) * g


def _const_spec(shape):
    nd = len(shape)
    return pl.BlockSpec(shape, lambda *_: (0,) * nd, pipeline_mode=pl.Buffered(1))


def _mla_prep_body(x_ref, g_ref, wm_ref, qn_ref, wuqT_ref, kvn_ref, wuk_ref, wuvT_ref,
                   cos_ref, sin_ref, cosT_ref, sinT_ref, qT_ref, k_ref, vT_ref, *, q_scale):
    xn = _rms(x_ref[0], g_ref[...]).astype(BF16)
    z = jnp.dot(xn, wm_ref[...], preferred_element_type=F32)
    cq = _rms(z[:, :MLA_Q_RANK], qn_ref[...]).astype(BF16)
    ckv = _rms(z[:, MLA_Q_RANK:MLA_Q_RANK + MLA_KV_RANK], kvn_ref[...]).astype(BF16)
    kr = z[:, MLA_Q_RANK + MLA_KV_RANK:]

    nt = (((1,), (1,)), ((), ()))
    qT = lax.dot_general(wuqT_ref[...], cq, nt, preferred_element_type=F32)
    cosT = cosT_ref[...]
    sinT = sinT_ref[...]
    half = MLA_ROPE // 2
    for h in range(MLA_HEADS):
        base = h * MLA_QK
        x1 = qT[base + MLA_NOPE:base + MLA_NOPE + half]
        x2 = qT[base + MLA_NOPE + half:base + MLA_QK]
        qT_ref[0, base:base + MLA_NOPE, :] = (qT[base:base + MLA_NOPE] * q_scale).astype(BF16)
        qT_ref[0, base + MLA_NOPE:base + MLA_NOPE + half, :] = ((x1 * cosT - x2 * sinT) * q_scale).astype(BF16)
        qT_ref[0, base + MLA_NOPE + half:base + MLA_QK, :] = ((x2 * cosT + x1 * sinT) * q_scale).astype(BF16)

    kn = jnp.dot(ckv, wuk_ref[...], preferred_element_type=F32)
    cos = cos_ref[...]
    sin = sin_ref[...]
    k1 = kr[:, :half]
    k2 = kr[:, half:]
    krope = jnp.concatenate([k1 * cos - k2 * sin, k2 * cos + k1 * sin], axis=-1).astype(BF16)
    for h in range(MLA_HEADS):
        k_ref[0, h, :, 0:MLA_NOPE] = kn[:, h * MLA_NOPE:(h + 1) * MLA_NOPE].astype(BF16)
        k_ref[0, h, :, MLA_NOPE:MLA_QK] = krope

    vT = lax.dot_general(wuvT_ref[...], ckv, nt, preferred_element_type=F32)
    for h in range(MLA_HEADS):
        vT_ref[0, h, 0] = vT[h * MLA_V:(h + 1) * MLA_V].astype(BF16)


def _mla_prep(x, g, wm, qn, wuqT, kvn, wuk, wuvT, cos, sin, cosT, sinT, blk):
    B, S, D = x.shape
    nb = S // blk
    half = MLA_ROPE // 2
    q_scale = (MLA_QK ** -0.5) * LOG2E
    return pl.pallas_call(
        functools.partial(_mla_prep_body, q_scale=q_scale),
        grid=(B, nb),
        in_specs=[
            pl.BlockSpec((1, blk, D), lambda b, i: (b, i, 0)),
            _const_spec(g.shape), _const_spec(wm.shape), _const_spec(qn.shape), _const_spec(wuqT.shape),
            _const_spec(kvn.shape), _const_spec(wuk.shape), _const_spec(wuvT.shape),
            pl.BlockSpec((blk, half), lambda b, i: (i, 0)),
            pl.BlockSpec((blk, half), lambda b, i: (i, 0)),
            pl.BlockSpec((half, blk), lambda b, i: (0, i)),
            pl.BlockSpec((half, blk), lambda b, i: (0, i)),
        ],
        out_specs=[
            pl.BlockSpec((1, MLA_HEADS * MLA_QK, blk), lambda b, i: (b, 0, i)),
            pl.BlockSpec((1, MLA_HEADS, blk, MLA_QK), lambda b, i: (b, 0, i, 0)),
            pl.BlockSpec((1, MLA_HEADS, 1, MLA_V, blk), lambda b, i: (b, 0, i, 0, 0)),
        ],
        out_shape=[
            jax.ShapeDtypeStruct((B, MLA_HEADS * MLA_QK, S), BF16),
            jax.ShapeDtypeStruct((B, MLA_HEADS, S, MLA_QK), BF16),
            jax.ShapeDtypeStruct((B, MLA_HEADS, nb, MLA_V, blk), BF16),
        ],
        compiler_params=_params("parallel", "parallel"),
        name="mla_prep",
    )(x, g, wm, qn, wuqT, kvn, wuk, wuvT, cos, sin, cosT, sinT)


def _attn_body(qT_ref, k_ref, vT_ref, o_ref, m_ref, l_ref, acc_ref, *, bq, bk):
    i = pl.program_id(2)
    ratio = bq // bk
    qT = qT_ref[0]
    m_ref[...] = jnp.full(m_ref.shape, NEG, F32)
    l_ref[...] = jnp.zeros(l_ref.shape, F32)
    acc_ref[...] = jnp.zeros(acc_ref.shape, F32)

    def update(j, c0, masked):
        kblk = k_ref[0, 0, pl.ds(pl.multiple_of(j * bk, bk), bk), :]
        s = jnp.dot(kblk, qT[:, c0:], preferred_element_type=F32)
        if masked:
            key_chunk = lax.broadcasted_iota(jnp.int32, s.shape, 0) // CHUNK
            qry_chunk = lax.broadcasted_iota(jnp.int32, s.shape, 1) // CHUNK
            s = jnp.where(key_chunk <= qry_chunk, s, NEG)
        m_prev = m_ref[:, c0:]
        m_new = jnp.maximum(m_prev, jnp.max(s, axis=0, keepdims=True))
        alpha = jnp.exp2(m_prev - m_new)
        p = jnp.exp2(s - m_new)
        l_ref[:, c0:] = alpha * l_ref[:, c0:] + jnp.sum(p, axis=0, keepdims=True)
        acc_ref[:, c0:] = alpha * acc_ref[:, c0:] + jnp.dot(vT_ref[0, 0, j], p.astype(BF16),
                                                            preferred_element_type=F32)
        m_ref[:, c0:] = m_new

    def body(j, carry):
        update(j, 0, False)
        return carry

    lax.fori_loop(0, i * ratio, body, 0)
    for d in range(ratio):
        update(i * ratio + d, d * bk, True)
    o_ref[0] = (acc_ref[...] / l_ref[...]).T


def _attention(qT, k, vT, bq, bk):
    B, H, S, _ = k.shape
    return pl.pallas_call(
        functools.partial(_attn_body, bq=bq, bk=bk),
        grid=(B, H, S // bq),
        in_specs=[
            pl.BlockSpec((1, MLA_QK, bq), lambda b, h, i: (b, h, i)),
            pl.BlockSpec((1, 1, S, MLA_QK), lambda b, h, i: (b, h, 0, 0)),
            pl.BlockSpec((1, 1, S // bk, MLA_V, bk), lambda b, h, i: (b, h, 0, 0, 0)),
        ],
        out_specs=pl.BlockSpec((1, bq, MLA_V), lambda b, h, i: (b, i, h)),
        out_shape=jax.ShapeDtypeStruct((B, S, H * MLA_V), F32),
        scratch_shapes=[pltpu.VMEM((1, bq), F32), pltpu.VMEM((1, bq), F32), pltpu.VMEM((MLA_V, bq), F32)],
        compiler_params=_params("parallel", "parallel", "arbitrary"),
        name="mla_attention",
    )(qT, k, vT)


def _norm_matmul_body(x_ref, g_ref, w_ref, o_ref, xn_ref):
    @pl.when(pl.program_id(1) == 0)
    def _():
        xn_ref[...] = _rms(x_ref[...], g_ref[...]).astype(BF16)

    o_ref[...] = jnp.dot(xn_ref[...], w_ref[...], preferred_element_type=F32).astype(o_ref.dtype)


def _norm_matmul(x2d, g, w, out_dtype, tm, tn, name):
    N, D = x2d.shape
    M = w.shape[1]
    return pl.pallas_call(
        _norm_matmul_body,
        grid=(N // tm, M // tn),
        in_specs=[
            pl.BlockSpec((tm, D), lambda i, j: (i, 0)),
            _const_spec(g.shape),
            pl.BlockSpec((D, tn), lambda i, j: (0, j)),
        ],
        out_specs=pl.BlockSpec((tm, tn), lambda i, j: (i, j)),
        out_shape=jax.ShapeDtypeStruct((N, M), out_dtype),
        scratch_shapes=[pltpu.VMEM((tm, D), BF16)],
        compiler_params=_params("parallel", "arbitrary"),
        name=name,
    )(x2d, g, w)


def _group_scan(x, row_in_group, group, reverse):
    n = x.shape[0]
    sh = 1
    while sh < group:
        if reverse:
            x = x + jnp.where(row_in_group < group - sh, pltpu.roll(x, n - sh, axis=0), 0.0)
        else:
            x = x + jnp.where(row_in_group >= sh, pltpu.roll(x, sh, axis=0), 0.0)
        sh *= 2
    return x


def _hgrn_body(hq_ref, hf_ref, hi_ref, hg_ref, lbp_ref, gain_ref, o_ref,
               st_ref, q_s, k_s, b_s, v_s, qd_s, kd_s, es_s, o_s, *, lc, layer):
    T = HG_SUB
    H = HG_HEADS
    half = T // 2

    @pl.when(pl.program_id(1) == 0)
    def _():
        st_ref[...] = jnp.zeros(st_ref.shape, F32)

    lbp = lbp_ref[...]
    e = jnp.exp(lbp - jnp.max(lbp, axis=0, keepdims=True))
    prob = e / jnp.sum(e, axis=0, keepdims=True)
    lb = jnp.sum(prob[:layer + 1], axis=0, keepdims=True)

    hq = hq_ref[0]
    q = hq * jax.nn.sigmoid(hq)
    f = lb + (1.0 - lb) * jax.nn.sigmoid(hf_ref[0])
    k = 1.0 - f
    logf = jnp.log(f)

    rig = lax.broadcasted_iota(jnp.int32, logf.shape, 0) % T
    b = _group_scan(logf, rig, T, reverse=False)
    suffix = _group_scan(logf, rig, T, reverse=True)
    staged = ((q_s, q), (k_s, k), (b_s, b), (v_s, hi_ref[0]), (qd_s, q * jnp.exp(b)),
              (kd_s, k * jnp.exp(suffix - logf)),
              (es_s, jnp.exp(suffix)))
    for ref, val in staged:
        for h in range(H):
            ref[h] = val[:, h * HG_DK:(h + 1) * HG_DK]

    row = lax.broadcasted_iota(jnp.int32, (half, HG_DK), 0)
    nt = (((1,), (1,)), ((), ()))
    tn = (((0,), (0,)), ((), ()))

    def sub(i, carry):
        r0 = pl.multiple_of(i * T, T)
        r1 = pl.multiple_of(i * T + half, half)
        for h in range(H):
            cs = slice(h * HG_DK, (h + 1) * HG_DK)
            q_lo = q_s[h, pl.ds(r0, half), :]
            q_hi = q_s[h, pl.ds(r1, half), :]
            b_lo = b_s[h, pl.ds(r0, half), :]
            b_hi = b_s[h, pl.ds(r1, half), :]
            acc_lo = jnp.zeros((half, HG_DV), F32)
            acc_hi = jnp.zeros((half, HG_DV), F32)
            for s in range(T):
                ks = k_s[h, pl.ds(r0 + s, 1), :]
                bs = b_s[h, pl.ds(r0 + s, 1), :]
                vs = v_s[h, pl.ds(r0 + s, 1), :]
                if s < half:
                    d_lo = jnp.exp(jnp.where(row >= s, b_lo - bs, NEG))
                    acc_lo = acc_lo + jnp.sum(q_lo * ks * d_lo, axis=-1, keepdims=True) * vs
                    d_hi = jnp.exp(b_hi - bs)
                else:
                    d_hi = jnp.exp(jnp.where(row >= s - half, b_hi - bs, NEG))
                acc_hi = acc_hi + jnp.sum(q_hi * ks * d_hi, axis=-1, keepdims=True) * vs
            st = st_ref[h]
            inter = lax.dot_general(qd_s[h, pl.ds(r0, T), :], st, nt, preferred_element_type=F32)
            o_s[h, pl.ds(r0, half), :] = acc_lo + inter[:half]
            o_s[h, pl.ds(r1, half), :] = acc_hi + inter[half:]
            upd = lax.dot_general(v_s[h, pl.ds(r0, T), :], kd_s[h, pl.ds(r0, T), :], tn,
                                  preferred_element_type=F32)
            st_ref[h] = st * es_s[h, pl.ds(r0, 1), :] + upd
        return carry

    lax.fori_loop(0, lc // T, sub, 0)

    gain = gain_ref[...]
    for h in range(H):
        cs = slice(h * HG_DV, (h + 1) * HG_DV)
        hg = hg_ref[0, :, cs]
        o_ref[0, :, cs] = _rms(o_s[h], gain) * (hg * jax.nn.sigmoid(hg))


def _hgrn(zh, lb_table, gain, layer, lc):
    B, S, _ = zh.shape
    W = HG_HEADS * HG_DK
    blk = lambda part: pl.BlockSpec((1, lc, W), lambda b, c: (b, c, part))
    scr = lambda: pltpu.VMEM((HG_HEADS, lc, HG_DK), F32)
    return pl.pallas_call(
        functools.partial(_hgrn_body, lc=lc, layer=layer),
        grid=(B, S // lc),
        in_specs=[
            blk(0), blk(1), blk(2), blk(3),
            pl.BlockSpec(lb_table.shape, lambda b, c: (0, 0)),
            pl.BlockSpec((1, HG_DV), lambda b, c: (0, 0)),
        ],
        out_specs=pl.BlockSpec((1, lc, W), lambda b, c: (b, c, 0)),
        out_shape=jax.ShapeDtypeStruct((B, S, W), F32),
        scratch_shapes=[pltpu.VMEM((HG_HEADS, HG_DV, HG_DK), F32)] + [scr() for _ in range(8)],
        compiler_params=_params("parallel", "arbitrary"),
        name="hgrn2",
    )(zh, zh, zh, zh, lb_table, gain)


def _mix_out_body(a_ref, r_ref, x_ref, gm_ref, wa_ref, wr_ref, gp_ref, o_ref):
    an = _rms(a_ref[...], gm_ref[...]).astype(BF16)
    y = jnp.dot(an, wa_ref[...], preferred_element_type=F32)
    y = y + jnp.dot(r_ref[...].astype(BF16), wr_ref[...], preferred_element_type=F32)
    o_ref[...] = x_ref[...] + _rms(y, gp_ref[...])


def _mix_out(a2d, r2d, x2d, gm, wa, wr, gp, tm):
    N, D = x2d.shape
    return pl.pallas_call(
        _mix_out_body,
        grid=(N // tm,),
        in_specs=[
            pl.BlockSpec((tm, a2d.shape[1]), lambda i: (i, 0)),
            pl.BlockSpec((tm, r2d.shape[1]), lambda i: (i, 0)),
            pl.BlockSpec((tm, D), lambda i: (i, 0)),
            _const_spec(gm.shape), _const_spec(wa.shape), _const_spec(wr.shape), _const_spec(gp.shape),
        ],
        out_specs=pl.BlockSpec((tm, D), lambda i: (i, 0)),
        out_shape=jax.ShapeDtypeStruct((N, D), F32),
        compiler_params=_params("parallel"),
        name="mix_out",
    )(a2d, r2d, x2d, gm, wa, wr, gp)


def _xattn_body(h_ref, gpre_ref, wq_ref, kv_ref, wo_ref, gpost_ref, o_ref, oh_ref, *, q_scale):
    h = h_ref[...]
    D = h.shape[-1]
    dh = D // X_HEADS
    xn = _rms(h, gpre_ref[...]).astype(BF16)
    qx = (jnp.dot(xn, wq_ref[...], preferred_element_type=F32) * q_scale).astype(BF16)
    for hd in range(X_HEADS):
        qh = qx[:, hd * dh:(hd + 1) * dh]
        kh = kv_ref[0, :, hd * dh:(hd + 1) * dh]
        vh = kv_ref[0, :, D + hd * dh:D + (hd + 1) * dh]
        s = lax.dot_general(qh, kh, (((1,), (1,)), ((), ())), preferred_element_type=F32)
        p = jnp.exp2(s - jnp.max(s, axis=-1, keepdims=True))
        l = jnp.sum(p, axis=-1, keepdims=True)
        oh = jnp.dot(p.astype(BF16), vh, preferred_element_type=F32) / l
        oh_ref[:, hd * dh:(hd + 1) * dh] = oh.astype(BF16)
    ox = jnp.dot(oh_ref[...], wo_ref[...], preferred_element_type=F32)
    o_ref[...] = h + _rms(ox, gpost_ref[...])


def _xattn(h2d, gpre, wq, kvm, wo, gpost, tm, tiles_per_batch):
    N, D = h2d.shape
    M = kvm.shape[1]
    q_scale = ((D // X_HEADS) ** -0.5) * LOG2E
    return pl.pallas_call(
        functools.partial(_xattn_body, q_scale=q_scale),
        grid=(N // tm,),
        in_specs=[
            pl.BlockSpec((tm, D), lambda i: (i, 0)),
            _const_spec(gpre.shape), _const_spec(wq.shape),
            pl.BlockSpec((1, M, 2 * D), lambda i: (i // tiles_per_batch, 0, 0)),
            _const_spec(wo.shape), _const_spec(gpost.shape),
        ],
        out_specs=pl.BlockSpec((tm, D), lambda i: (i, 0)),
        out_shape=jax.ShapeDtypeStruct((N, D), F32),
        scratch_shapes=[pltpu.VMEM((tm, D), BF16)],
        compiler_params=_params("parallel"),
        name="xattn",
    )(h2d, gpre, wq, kvm, wo, gpost)


def _gelu_tanh(x):
    return x * (0.5 * (1.0 + jnp.tanh(math.sqrt(2.0 / math.pi) * (x + 0.044715 * (x * x * x)))))


def _ffn_up_body(h_ref, gpre_ref, wg_ref, wv_ref, cwg_ref, cwv_ref, cbg_ref, cbv_ref,
                 act_ref, xn_ref, cg_ref, cv_ref, bg_ref, bv_ref, *, tm, tiles_per_batch):
    i = pl.program_id(0)
    j = pl.program_id(1)

    @pl.when(j == 0)
    def _():
        xn_ref[...] = _rms(h_ref[...], gpre_ref[...]).astype(BF16)

    first = (i % tiles_per_batch) == 0

    def project(w_ref, carry_ref, buf_ref):
        u = jnp.dot(xn_ref[...], w_ref[...], preferred_element_type=F32)

        @pl.when(first)
        def _():
            buf_ref[0:CONV_PAD, :] = jnp.zeros((CONV_PAD, u.shape[1]), F32)

        @pl.when(jnp.logical_not(first))
        def _():
            buf_ref[0:CONV_PAD, :] = carry_ref[j]

        buf_ref[CONV_PAD:CONV_PAD + tm, :] = u
        carry_ref[j] = u[tm - CONV_PAD:, :]

    project(wg_ref, cg_ref, bg_ref)
    project(wv_ref, cv_ref, bv_ref)

    cwg = cwg_ref[...]
    cwv = cwv_ref[...]
    cbg = cbg_ref[...]
    cbv = cbv_ref[...]

    def conv(buf_ref, cw, cb, r):
        base = CONV_PAD + r
        return (cw[0:1, :] * buf_ref[base - 2:base - 2 + FF_CHUNK, :]
                + cw[1:2, :] * buf_ref[base - 1:base - 1 + FF_CHUNK, :]
                + cw[2:3, :] * buf_ref[base:base + FF_CHUNK, :] + cb)

    for r in range(0, tm, FF_CHUNK):
        gate = conv(bg_ref, cwg, cbg, r)
        val = conv(bv_ref, cwv, cbv, r)
        act_ref[r:r + FF_CHUNK, :] = (_gelu_tanh(gate) * val).astype(BF16)


def _ffn_up(h2d, gpre, wg, wv, cwg, cwv, cbg, cbv, tm, tf, tiles_per_batch):
    N, D = h2d.shape
    F = wg.shape[1]
    nj = F // tf
    col = lambda r: pl.BlockSpec((r, tf), lambda i, j: (0, j))
    return pl.pallas_call(
        functools.partial(_ffn_up_body, tm=tm, tiles_per_batch=tiles_per_batch),
        grid=(N // tm, nj),
        in_specs=[
            pl.BlockSpec((tm, D), lambda i, j: (i, 0)),
            _const_spec(gpre.shape),
            pl.BlockSpec((D, tf), lambda i, j: (0, j)),
            pl.BlockSpec((D, tf), lambda i, j: (0, j)),
            col(CONV_WIDTH), col(CONV_WIDTH), col(1), col(1),
        ],
        out_specs=pl.BlockSpec((tm, tf), lambda i, j: (i, j)),
        out_shape=jax.ShapeDtypeStruct((N, F), BF16),
        scratch_shapes=[
            pltpu.VMEM((tm, D), BF16),
            pltpu.VMEM((nj, CONV_PAD, tf), F32),
            pltpu.VMEM((nj, CONV_PAD, tf), F32),
            pltpu.VMEM((CONV_PAD + tm, tf), F32),
            pltpu.VMEM((CONV_PAD + tm, tf), F32),
        ],
        compiler_params=_params("arbitrary", "arbitrary"),
        name="ffn_up",
    )(h2d, gpre, wg, wv, cwg, cwv, cbg, cbv)


def _ffn_down_body(act_ref, h_ref, wd_ref, gpost_ref, o_ref):
    y = jnp.dot(act_ref[...], wd_ref[...], preferred_element_type=F32)
    o_ref[...] = h_ref[...] + _rms(y, gpost_ref[...])


def _ffn_down(act, h2d, wd, gpost, tm):
    N, D = h2d.shape
    F = act.shape[1]
    return pl.pallas_call(
        _ffn_down_body,
        grid=(N // tm,),
        in_specs=[
            pl.BlockSpec((tm, F), lambda i: (i, 0)),
            pl.BlockSpec((tm, D), lambda i: (i, 0)),
            _const_spec(wd.shape), _const_spec(gpost.shape),
        ],
        out_specs=pl.BlockSpec((tm, D), lambda i: (i, 0)),
        out_shape=jax.ShapeDtypeStruct((N, D), F32),
        compiler_params=_params("parallel"),
        name="ffn_down",
    )(act, h2d, wd, gpost)


def kernel(x, mem, w_in, q_norm, w_uq, kv_norm, w_ukv, mla_out_norm, hgrn_lb, hgrn_out_norm, w_out,
           ln_mix_pre, ln_mix_post, ln_x_pre, ln_x_post, mem_norm, w_xq, w_xk, w_xv, w_xo,
           ln_ffn_pre, ln_ffn_post, w_up, conv_w, conv_b, w_down):
    B, S, D = x.shape
    M = mem.shape[1]
    depth = w_in.shape[0]
    N = B * S
    tm = min(ROW_TILE, S)
    blk = min(ATT_BLK, S)
    bq = min(ATT_QBLK, S)
    lc = min(HG_BLK, S)
    assert S % tm == 0 and S % blk == 0 and S % bq == 0 and bq % blk == 0 and blk % CHUNK == 0
    assert S % lc == 0 and lc % HG_SUB == 0
    tiles_per_batch = S // tm
    d_ff = w_down.shape[1]
    tf = FF_TILE
    tmf = min(FF_ROWS, S)
    assert d_ff % tf == 0 and S % tmf == 0 and tmf % FF_CHUNK == 0
    n_mla_in = MLA_Q_RANK + MLA_KV_RANK + MLA_ROPE

    pos = jnp.arange(S, dtype=F32)
    inv_freq = 1.0 / (ROPE_THETA ** (jnp.arange(0, MLA_ROPE, 2, dtype=F32) / MLA_ROPE))
    ang = pos[:, None] * inv_freq[None, :]
    cos, sin = jnp.cos(ang), jnp.sin(ang)
    cosT, sinT = cos.T, sin.T

    row = lambda v: v.reshape(1, -1).astype(F32)
    h = x
    for l in range(depth):
        w_mla = w_in[l][:, :n_mla_in].astype(BF16)
        w_hg = w_in[l][:, n_mla_in:].astype(BF16)
        w_uqT = w_uq[l].T.astype(BF16)
        w_ukv_h = w_ukv[l].reshape(MLA_KV_RANK, MLA_HEADS, MLA_NOPE + MLA_V)
        w_uk = w_ukv_h[:, :, :MLA_NOPE].reshape(MLA_KV_RANK, MLA_HEADS * MLA_NOPE).astype(BF16)
        w_uvT = w_ukv_h[:, :, MLA_NOPE:].reshape(MLA_KV_RANK, MLA_HEADS * MLA_V).T.astype(BF16)
        w_out_a = w_out[l][:MLA_WIDTH].astype(BF16)
        w_out_r = w_out[l][MLA_WIDTH:].astype(BF16)
        w_xkv = jnp.concatenate([w_xk[l], w_xv[l]], axis=1).astype(BF16)
        w_up_g = w_up[l][:, :d_ff].astype(BF16)
        w_up_v = w_up[l][:, d_ff:].astype(BF16)

        qT, k, vT = _mla_prep(h, row(ln_mix_pre[l]), w_mla, row(q_norm[l]), w_uqT, row(kv_norm[l]),
                              w_uk, w_uvT, cos, sin, cosT, sinT, blk)
        a = _attention(qT, k, vT, bq, blk)
        zh = _norm_matmul(h.reshape(N, D), row(ln_mix_pre[l]), w_hg, F32, tm, 1024, "hgrn_in")
        r = _hgrn(zh.reshape(B, S, -1), hgrn_lb.astype(F32), row(hgrn_out_norm[l]), l, lc)
        h2d = _mix_out(a.reshape(N, -1), r.reshape(N, -1), h.reshape(N, D), row(mla_out_norm[l]),
                       w_out_a, w_out_r, row(ln_mix_post[l]), tm)

        kvm = _norm_matmul(mem.reshape(B * M, D), row(mem_norm[l]), w_xkv, BF16, min(256, B * M), 1024, "mem_kv")
        h2d = _xattn(h2d, row(ln_x_pre[l]), w_xq[l].astype(BF16), kvm.reshape(B, M, 2 * D),
                     w_xo[l].astype(BF16), row(ln_x_post[l]), tm, tiles_per_batch)

        act = _ffn_up(h2d, row(ln_ffn_pre[l]), w_up_g, w_up_v, conv_w[l][:, :d_ff], conv_w[l][:, d_ff:],
                      row(conv_b[l][:d_ff]), row(conv_b[l][d_ff:]), tmf, tf, S // tmf)
        h2d = _ffn_down(act, h2d, w_down[l].astype(BF16), row(ln_ffn_post[l]), min(FF_DOWN_ROWS, S))
        h = h2d.reshape(B, S, D)
    return h
```

```python
import functools
import math

import jax
import jax.numpy as jnp
from jax import lax
from jax.experimental import pallas as pl
from jax.experimental.pallas import tpu as pltpu

F32 = jnp.float32
BF16 = jnp.bfloat16

EPS = 1e-6
CHUNK = 64
MLA_V = 128
MLA_NOPE = 128
MLA_ROPE = 64
MLA_HEADS = 8
MLA_QK = MLA_NOPE + MLA_ROPE
MLA_Q_RANK = 512
MLA_KV_RANK = 256
ROPE_THETA = 10000.0
HG_DK = 128
HG_DV = 128
HG_HEADS = 8
HG_WIDTH = HG_HEADS * HG_DV
MLA_WIDTH = MLA_HEADS * MLA_V
X_HEADS = 4
CONV_WIDTH = 3

LOG2E = 1.4426950408889634
NEG = -1e30

VMEM_LIMIT_BYTES = 56 * 1024 * 1024

ROW_TILE = 512
ATT_BLK = 512
ATT_QBLK = 2048
ATT_MIN_MASS = 2.0 ** -80
HG_BLK = 256
HG_SUB = 16
FF_TILE = 512
FF_ROWS = 1024
FF_CHUNK = 64
FF_DOWN_ROWS = 256
CONV_PAD = 8


def _params(*sem):
    return pltpu.CompilerParams(dimension_semantics=sem, vmem_limit_bytes=VMEM_LIMIT_BYTES)


def _rms(xf, g):
    return xf * lax.rsqrt(jnp.mean(xf * xf, axis=-1, keepdims=True) + EPS) * g


def _const_spec(shape):
    nd = len(shape)
    return pl.BlockSpec(shape, lambda *_: (0,) * nd, pipeline_mode=pl.Buffered(1))


def _mla_prep_body(x_ref, g_ref, wm_ref, qn_ref, wuqT_ref, kvn_ref, wuk_ref, wuvT_ref,
                   cos_ref, sin_ref, cosT_ref, sinT_ref, qT_ref, k_ref, vT_ref, *, q_scale):
    xn = _rms(x_ref[0], g_ref[...]).astype(BF16)
    z = jnp.dot(xn, wm_ref[...], preferred_element_type=F32)
    cq = _rms(z[:, :MLA_Q_RANK], qn_ref[...]).astype(BF16)
    ckv = _rms(z[:, MLA_Q_RANK:MLA_Q_RANK + MLA_KV_RANK], kvn_ref[...]).astype(BF16)
    kr = z[:, MLA_Q_RANK + MLA_KV_RANK:]

    nt = (((1,), (1,)), ((), ()))
    qT = lax.dot_general(wuqT_ref[...], cq, nt, preferred_element_type=F32)
    cosT = cosT_ref[...]
    sinT = sinT_ref[...]
    half = MLA_ROPE // 2
    for h in range(MLA_HEADS):
        base = h * MLA_QK
        x1 = qT[base + MLA_NOPE:base + MLA_NOPE + half]
        x2 = qT[base + MLA_NOPE + half:base + MLA_QK]
        qT_ref[0, base:base + MLA_NOPE, :] = (qT[base:base + MLA_NOPE] * q_scale).astype(BF16)
        qT_ref[0, base + MLA_NOPE:base + MLA_NOPE + half, :] = ((x1 * cosT - x2 * sinT) * q_scale).astype(BF16)
        qT_ref[0, base + MLA_NOPE + half:base + MLA_QK, :] = ((x2 * cosT + x1 * sinT) * q_scale).astype(BF16)

    kn = jnp.dot(ckv, wuk_ref[...], preferred_element_type=F32)
    cos = cos_ref[...]
    sin = sin_ref[...]
    k1 = kr[:, :half]
    k2 = kr[:, half:]
    krope = jnp.concatenate([k1 * cos - k2 * sin, k2 * cos + k1 * sin], axis=-1).astype(BF16)
    for h in range(MLA_HEADS):
        k_ref[0, h, :, 0:MLA_NOPE] = kn[:, h * MLA_NOPE:(h + 1) * MLA_NOPE].astype(BF16)
        k_ref[0, h, :, MLA_NOPE:MLA_QK] = krope

    vT = lax.dot_general(wuvT_ref[...], ckv, nt, preferred_element_type=F32)
    for h in range(MLA_HEADS):
        vT_ref[0, h, 0] = vT[h * MLA_V:(h + 1) * MLA_V].astype(BF16)


def _mla_prep(x, g, wm, qn, wuqT, kvn, wuk, wuvT, cos, sin, cosT, sinT, blk):
    B, S, D = x.shape
    nb = S // blk
    half = MLA_ROPE // 2
    q_scale = (MLA_QK ** -0.5) * LOG2E
    return pl.pallas_call(
        functools.partial(_mla_prep_body, q_scale=q_scale),
        grid=(B, nb),
        in_specs=[
            pl.BlockSpec((1, blk, D), lambda b, i: (b, i, 0)),
            _const_spec(g.shape), _const_spec(wm.shape), _const_spec(qn.shape), _const_spec(wuqT.shape),
            _const_spec(kvn.shape), _const_spec(wuk.shape), _const_spec(wuvT.shape),
            pl.BlockSpec((blk, half), lambda b, i: (i, 0)),
            pl.BlockSpec((blk, half), lambda b, i: (i, 0)),
            pl.BlockSpec((half, blk), lambda b, i: (0, i)),
            pl.BlockSpec((half, blk), lambda b, i: (0, i)),
        ],
        out_specs=[
            pl.BlockSpec((1, MLA_HEADS * MLA_QK, blk), lambda b, i: (b, 0, i)),
            pl.BlockSpec((1, MLA_HEADS, blk, MLA_QK), lambda b, i: (b, 0, i, 0)),
            pl.BlockSpec((1, MLA_HEADS, 1, MLA_V, blk), lambda b, i: (b, 0, i, 0, 0)),
        ],
        out_shape=[
            jax.ShapeDtypeStruct((B, MLA_HEADS * MLA_QK, S), BF16),
            jax.ShapeDtypeStruct((B, MLA_HEADS, S, MLA_QK), BF16),
            jax.ShapeDtypeStruct((B, MLA_HEADS, nb, MLA_V, blk), BF16),
        ],
        compiler_params=_params("parallel", "parallel"),
        name="mla_prep",
    )(x, g, wm, qn, wuqT, kvn, wuk, wuvT, cos, sin, cosT, sinT)


def _attn_body(qT_ref, k_ref, vT_ref, o_ref, kn_ref, m_ref, l_ref, acc_ref, *, bq, bk):
    i = pl.program_id(2)
    ratio = bq // bk
    qT = qT_ref[0]

    @pl.when(i == 0)
    def _():
        def kmax(c, best):
            kc = k_ref[0, 0, pl.ds(pl.multiple_of(c * bk, bk), bk), :].astype(F32)
            sq = jnp.sum(kc * kc, axis=-1, keepdims=True)
            return jnp.maximum(best, jnp.max(sq, axis=0, keepdims=True))

        best = lax.fori_loop(0, k_ref.shape[2] // bk, kmax, jnp.zeros((1, 1), F32))
        kn_ref[...] = jnp.broadcast_to(jnp.sqrt(best), kn_ref.shape)

    def scores(j, c0, masked):
        kblk = k_ref[0, 0, pl.ds(pl.multiple_of(j * bk, bk), bk), :]
        s = jnp.dot(kblk, qT[:, c0:], preferred_element_type=F32)
        if masked:
            key_chunk = lax.broadcasted_iota(jnp.int32, s.shape, 0) // CHUNK
            qry_chunk = lax.broadcasted_iota(jnp.int32, s.shape, 1) // CHUNK
            s = jnp.where(key_chunk <= qry_chunk, s, NEG)
        return s

    def sweep(update):
        l_ref[...] = jnp.zeros(l_ref.shape, F32)
        acc_ref[...] = jnp.zeros(acc_ref.shape, F32)

        def body(j, carry):
            update(j, 0, False)
            return carry

        lax.fori_loop(0, i * ratio, body, 0)
        for d in range(ratio):
            update(i * ratio + d, d * bk, True)

    def fixed_update(j, c0, masked):
        p = jnp.exp2(scores(j, c0, masked) - m_ref[:, c0:])
        l_ref[:, c0:] += jnp.sum(p, axis=0, keepdims=True)
        acc_ref[:, c0:] += jnp.dot(vT_ref[0, 0, j], p.astype(BF16), preferred_element_type=F32)

    qf = qT.astype(F32)
    m_ref[...] = jnp.sqrt(jnp.sum(qf * qf, axis=0, keepdims=True)) * kn_ref[...] + 1.0
    sweep(fixed_update)

    def online_update(j, c0, masked):
        s = scores(j, c0, masked)
        m_prev = m_ref[:, c0:]
        m_new = jnp.maximum(m_prev, jnp.max(s, axis=0, keepdims=True))
        alpha = jnp.exp2(m_prev - m_new)
        p = jnp.exp2(s - m_new)
        l_ref[:, c0:] = alpha * l_ref[:, c0:] + jnp.sum(p, axis=0, keepdims=True)
        acc_ref[:, c0:] = alpha * acc_ref[:, c0:] + jnp.dot(vT_ref[0, 0, j], p.astype(BF16),
                                                            preferred_element_type=F32)
        m_ref[:, c0:] = m_new

    @pl.when(jnp.logical_not(jnp.min(l_ref[...]) >= ATT_MIN_MASS))
    def _():
        m_ref[...] = jnp.full(m_ref.shape, NEG, F32)
        sweep(online_update)

    o_ref[0] = (acc_ref[...] / l_ref[...]).T


def _attention(qT, k, vT, bq, bk):
    B, H, S, _ = k.shape
    return pl.pallas_call(
        functools.partial(_attn_body, bq=bq, bk=bk),
        grid=(B, H, S // bq),
        in_specs=[
            pl.BlockSpec((1, MLA_QK, bq), lambda b, h, i: (b, h, i)),
            pl.BlockSpec((1, 1, S, MLA_QK), lambda b, h, i: (b, h, 0, 0)),
            pl.BlockSpec((1, 1, S // bk, MLA_V, bk), lambda b, h, i: (b, h, 0, 0, 0)),
        ],
        out_specs=pl.BlockSpec((1, bq, MLA_V), lambda b, h, i: (b, i, h)),
        out_shape=jax.ShapeDtypeStruct((B, S, H * MLA_V), F32),
        scratch_shapes=[pltpu.VMEM((1, bq), F32), pltpu.VMEM((1, bq), F32), pltpu.VMEM((1, bq), F32),
                        pltpu.VMEM((MLA_V, bq), F32)],
        compiler_params=_params("parallel", "parallel", "arbitrary"),
        name="mla_attention",
    )(qT, k, vT)


def _norm_matmul_body(x_ref, g_ref, w_ref, o_ref, xn_ref):
    @pl.when(pl.program_id(1) == 0)
    def _():
        xn_ref[...] = _rms(x_ref[...], g_ref[...]).astype(BF16)

    o_ref[...] = jnp.dot(xn_ref[...], w_ref[...], preferred_element_type=F32).astype(o_ref.dtype)


def _norm_matmul(x2d, g, w, out_dtype, tm, tn, name):
    N, D = x2d.shape
    M = w.shape[1]
    return pl.pallas_call(
        _norm_matmul_body,
        grid=(N // tm, M // tn),
        in_specs=[
            pl.BlockSpec((tm, D), lambda i, j: (i, 0)),
            _const_spec(g.shape),
            pl.BlockSpec((D, tn), lambda i, j: (0, j)),
        ],
        out_specs=pl.BlockSpec((tm, tn), lambda i, j: (i, j)),
        out_shape=jax.ShapeDtypeStruct((N, M), out_dtype),
        scratch_shapes=[pltpu.VMEM((tm, D), BF16)],
        compiler_params=_params("parallel", "arbitrary"),
        name=name,
    )(x2d, g, w)


def _group_scan(x, row_in_group, group, reverse):
    n = x.shape[0]
    sh = 1
    while sh < group:
        if reverse:
            x = x + jnp.where(row_in_group < group - sh, pltpu.roll(x, n - sh, axis=0), 0.0)
        else:
            x = x + jnp.where(row_in_group >= sh, pltpu.roll(x, sh, axis=0), 0.0)
        sh *= 2
    return x


def _hgrn_body(hq_ref, hf_ref, hi_ref, hg_ref, lbp_ref, gain_ref, o_ref,
               st_ref, q_s, k_s, b_s, v_s, qd_s, kd_s, es_s, o_s, *, lc, layer):
    T = HG_SUB
    H = HG_HEADS
    half = T // 2

    @pl.when(pl.program_id(1) == 0)
    def _():
        st_ref[...] = jnp.zeros(st_ref.shape, F32)

    lbp = lbp_ref[...]
    e = jnp.exp(lbp - jnp.max(lbp, axis=0, keepdims=True))
    prob = e / jnp.sum(e, axis=0, keepdims=True)
    lb = jnp.sum(prob[:layer + 1], axis=0, keepdims=True)

    hq = hq_ref[0]
    q = hq * jax.nn.sigmoid(hq)
    f = lb + (1.0 - lb) * jax.nn.sigmoid(hf_ref[0])
    k = 1.0 - f
    logf = jnp.log(f)

    rig = lax.broadcasted_iota(jnp.int32, logf.shape, 0) % T
    b = _group_scan(logf, rig, T, reverse=False)
    suffix = _group_scan(logf, rig, T, reverse=True)
    staged = ((q_s, q), (k_s, k), (b_s, b), (v_s, hi_ref[0]), (qd_s, q * jnp.exp(b)),
              (kd_s, k * jnp.exp(suffix - logf)),
              (es_s, jnp.exp(suffix)))
    for ref, val in staged:
        for h in range(H):
            ref[h] = val[:, h * HG_DK:(h + 1) * HG_DK]

    row = lax.broadcasted_iota(jnp.int32, (half, HG_DK), 0)
    nt = (((1,), (1,)), ((), ()))
    tn = (((0,), (0,)), ((), ()))

    def sub(i, carry):
        r0 = pl.multiple_of(i * T, T)
        r1 = pl.multiple_of(i * T + half, half)
        for h in range(H):
            cs = slice(h * HG_DK, (h + 1) * HG_DK)
            q_lo = q_s[h, pl.ds(r0, half), :]
            q_hi = q_s[h, pl.ds(r1, half), :]
            b_lo = b_s[h, pl.ds(r0, half), :]
            b_hi = b_s[h, pl.ds(r1, half), :]
            acc_lo = jnp.zeros((half, HG_DV), F32)
            acc_hi = jnp.zeros((half, HG_DV), F32)
            for s in range(T):
                ks = k_s[h, pl.ds(r0 + s, 1), :]
                bs = b_s[h, pl.ds(r0 + s, 1), :]
                vs = v_s[h, pl.ds(r0 + s, 1), :]
                if s < half:
                    d_lo = jnp.exp(jnp.where(row >= s, b_lo - bs, NEG))
                    acc_lo = acc_lo + jnp.sum(q_lo * ks * d_lo, axis=-1, keepdims=True) * vs
                    d_hi = jnp.exp(b_hi - bs)
                else:
                    d_hi = jnp.exp(jnp.where(row >= s - half, b_hi - bs, NEG))
                acc_hi = acc_hi + jnp.sum(q_hi * ks * d_hi, axis=-1, keepdims=True) * vs
            st = st_ref[h]
            inter = lax.dot_general(qd_s[h, pl.ds(r0, T), :], st, nt, preferred_element_type=F32)
            o_s[h, pl.ds(r0, half), :] = acc_lo + inter[:half]
            o_s[h, pl.ds(r1, half), :] = acc_hi + inter[half:]
            upd = lax.dot_general(v_s[h, pl.ds(r0, T), :], kd_s[h, pl.ds(r0, T), :], tn,
                                  preferred_element_type=F32)
            st_ref[h] = st * es_s[h, pl.ds(r0, 1), :] + upd
        return carry

    lax.fori_loop(0, lc // T, sub, 0)

    gain = gain_ref[...]
    for h in range(H):
        cs = slice(h * HG_DV, (h + 1) * HG_DV)
        hg = hg_ref[0, :, cs]
        o_ref[0, :, cs] = _rms(o_s[h], gain) * (hg * jax.nn.sigmoid(hg))


def _hgrn(zh, lb_table, gain, layer, lc):
    B, S, _ = zh.shape
    W = HG_HEADS * HG_DK
    blk = lambda part: pl.BlockSpec((1, lc, W), lambda b, c: (b, c, part))
    scr = lambda: pltpu.VMEM((HG_HEADS, lc, HG_DK), F32)
    return pl.pallas_call(
        functools.partial(_hgrn_body, lc=lc, layer=layer),
        grid=(B, S // lc),
        in_specs=[
            blk(0), blk(1), blk(2), blk(3),
            pl.BlockSpec(lb_table.shape, lambda b, c: (0, 0)),
            pl.BlockSpec((1, HG_DV), lambda b, c: (0, 0)),
        ],
        out_specs=pl.BlockSpec((1, lc, W), lambda b, c: (b, c, 0)),
        out_shape=jax.ShapeDtypeStruct((B, S, W), F32),
        scratch_shapes=[pltpu.VMEM((HG_HEADS, HG_DV, HG_DK), F32)] + [scr() for _ in range(8)],
        compiler_params=_params("parallel", "arbitrary"),
        name="hgrn2",
    )(zh, zh, zh, zh, lb_table, gain)


def _mix_out_body(a_ref, r_ref, x_ref, gm_ref, wa_ref, wr_ref, gp_ref, o_ref):
    an = _rms(a_ref[...], gm_ref[...]).astype(BF16)
    y = jnp.dot(an, wa_ref[...], preferred_element_type=F32)
    y = y + jnp.dot(r_ref[...].astype(BF16), wr_ref[...], preferred_element_type=F32)
    o_ref[...] = x_ref[...] + _rms(y, gp_ref[...])


def _mix_out(a2d, r2d, x2d, gm, wa, wr, gp, tm):
    N, D = x2d.shape
    return pl.pallas_call(
        _mix_out_body,
        grid=(N // tm,),
        in_specs=[
            pl.BlockSpec((tm, a2d.shape[1]), lambda i: (i, 0)),
            pl.BlockSpec((tm, r2d.shape[1]), lambda i: (i, 0)),
            pl.BlockSpec((tm, D), lambda i: (i, 0)),
            _const_spec(gm.shape), _const_spec(wa.shape), _const_spec(wr.shape), _const_spec(gp.shape),
        ],
        out_specs=pl.BlockSpec((tm, D), lambda i: (i, 0)),
        out_shape=jax.ShapeDtypeStruct((N, D), F32),
        compiler_params=_params("parallel"),
        name="mix_out",
    )(a2d, r2d, x2d, gm, wa, wr, gp)


def _xattn_body(h_ref, gpre_ref, wq_ref, kv_ref, wo_ref, gpost_ref, o_ref, oh_ref, *, q_scale):
    h = h_ref[...]
    D = h.shape[-1]
    dh = D // X_HEADS
    xn = _rms(h, gpre_ref[...]).astype(BF16)
    qx = (jnp.dot(xn, wq_ref[...], preferred_element_type=F32) * q_scale).astype(BF16)
    for hd in range(X_HEADS):
        qh = qx[:, hd * dh:(hd + 1) * dh]
        kh = kv_ref[0, :, hd * dh:(hd + 1) * dh]
        vh = kv_ref[0, :, D + hd * dh:D + (hd + 1) * dh]
        s = lax.dot_general(qh, kh, (((1,), (1,)), ((), ())), preferred_element_type=F32)
        p = jnp.exp2(s - jnp.max(s, axis=-1, keepdims=True))
        l = jnp.sum(p, axis=-1, keepdims=True)
        oh = jnp.dot(p.astype(BF16), vh, preferred_element_type=F32) / l
        oh_ref[:, hd * dh:(hd + 1) * dh] = oh.astype(BF16)
    ox = jnp.dot(oh_ref[...], wo_ref[...], preferred_element_type=F32)
    o_ref[...] = h + _rms(ox, gpost_ref[...])


def _xattn(h2d, gpre, wq, kvm, wo, gpost, tm, tiles_per_batch):
    N, D = h2d.shape
    M = kvm.shape[1]
    q_scale = ((D // X_HEADS) ** -0.5) * LOG2E
    return pl.pallas_call(
        functools.partial(_xattn_body, q_scale=q_scale),
        grid=(N // tm,),
        in_specs=[
            pl.BlockSpec((tm, D), lambda i: (i, 0)),
            _const_spec(gpre.shape), _const_spec(wq.shape),
            pl.BlockSpec((1, M, 2 * D), lambda i: (i // tiles_per_batch, 0, 0)),
            _const_spec(wo.shape), _const_spec(gpost.shape),
        ],
        out_specs=pl.BlockSpec((tm, D), lambda i: (i, 0)),
        out_shape=jax.ShapeDtypeStruct((N, D), F32),
        scratch_shapes=[pltpu.VMEM((tm, D), BF16)],
        compiler_params=_params("parallel"),
        name="xattn",
    )(h2d, gpre, wq, kvm, wo, gpost)


def _gelu_tanh(x):
    return x * (0.5 * (1.0 + jnp.tanh(math.sqrt(2.0 / math.pi) * (x + 0.044715 * (x * x * x)))))


def _ffn_up_body(h_ref, gpre_ref, wg_ref, wv_ref, cwg_ref, cwv_ref, cbg_ref, cbv_ref,
                 act_ref, xn_ref, cg_ref, cv_ref, bg_ref, bv_ref, *, tm, tiles_per_batch):
    i = pl.program_id(0)
    j = pl.program_id(1)

    @pl.when(j == 0)
    def _():
        xn_ref[...] = _rms(h_ref[...], gpre_ref[...]).astype(BF16)

    first = (i % tiles_per_batch) == 0

    def project(w_ref, carry_ref, buf_ref):
        u = jnp.dot(xn_ref[...], w_ref[...], preferred_element_type=F32)

        @pl.when(first)
        def _():
            buf_ref[0:CONV_PAD, :] = jnp.zeros((CONV_PAD, u.shape[1]), F32)

        @pl.when(jnp.logical_not(first))
        def _():
            buf_ref[0:CONV_PAD, :] = carry_ref[j]

        buf_ref[CONV_PAD:CONV_PAD + tm, :] = u
        carry_ref[j] = u[tm - CONV_PAD:, :]

    project(wg_ref, cg_ref, bg_ref)
    project(wv_ref, cv_ref, bv_ref)

    cwg = cwg_ref[...]
    cwv = cwv_ref[...]
    cbg = cbg_ref[...]
    cbv = cbv_ref[...]

    def conv(buf_ref, cw, cb, r):
        base = CONV_PAD + r
        return (cw[0:1, :] * buf_ref[base - 2:base - 2 + FF_CHUNK, :]
                + cw[1:2, :] * buf_ref[base - 1:base - 1 + FF_CHUNK, :]
                + cw[2:3, :] * buf_ref[base:base + FF_CHUNK, :] + cb)

    for r in range(0, tm, FF_CHUNK):
        gate = conv(bg_ref, cwg, cbg, r)
        val = conv(bv_ref, cwv, cbv, r)
        act_ref[r:r + FF_CHUNK, :] = (_gelu_tanh(gate) * val).astype(BF16)


def _ffn_up(h2d, gpre, wg, wv, cwg, cwv, cbg, cbv, tm, tf, tiles_per_batch):
    N, D = h2d.shape
    F = wg.shape[1]
    nj = F // tf
    col = lambda r: pl.BlockSpec((r, tf), lambda i, j: (0, j))
    return pl.pallas_call(
        functools.partial(_ffn_up_body, tm=tm, tiles_per_batch=tiles_per_batch),
        grid=(N // tm, nj),
        in_specs=[
            pl.BlockSpec((tm, D), lambda i, j: (i, 0)),
            _const_spec(gpre.shape),
            pl.BlockSpec((D, tf), lambda i, j: (0, j)),
            pl.BlockSpec((D, tf), lambda i, j: (0, j)),
            col(CONV_WIDTH), col(CONV_WIDTH), col(1), col(1),
        ],
        out_specs=pl.BlockSpec((tm, tf), lambda i, j: (i, j)),
        out_shape=jax.ShapeDtypeStruct((N, F), BF16),
        scratch_shapes=[
            pltpu.VMEM((tm, D), BF16),
            pltpu.VMEM((nj, CONV_PAD, tf), F32),
            pltpu.VMEM((nj, CONV_PAD, tf), F32),
            pltpu.VMEM((CONV_PAD + tm, tf), F32),
            pltpu.VMEM((CONV_PAD + tm, tf), F32),
        ],
        compiler_params=_params("arbitrary", "arbitrary"),
        name="ffn_up",
    )(h2d, gpre, wg, wv, cwg, cwv, cbg, cbv)


def _ffn_down_body(act_ref, h_ref, wd_ref, gpost_ref, o_ref):
    y = jnp.dot(act_ref[...], wd_ref[...], preferred_element_type=F32)
    o_ref[...] = h_ref[...] + _rms(y, gpost_ref[...])


def _ffn_down(act, h2d, wd, gpost, tm):
    N, D = h2d.shape
    F = act.shape[1]
    return pl.pallas_call(
        _ffn_down_body,
        grid=(N // tm,),
        in_specs=[
            pl.BlockSpec((tm, F), lambda i: (i, 0)),
            pl.BlockSpec((tm, D), lambda i: (i, 0)),
            _const_spec(wd.shape), _const_spec(gpost.shape),
        ],
        out_specs=pl.BlockSpec((tm, D), lambda i: (i, 0)),
        out_shape=jax.ShapeDtypeStruct((N, D), F32),
        compiler_params=_params("parallel"),
        name="ffn_down",
    )(act, h2d, wd, gpost)


def kernel(x, mem, w_in, q_norm, w_uq, kv_norm, w_ukv, mla_out_norm, hgrn_lb, hgrn_out_norm, w_out,
           ln_mix_pre, ln_mix_post, ln_x_pre, ln_x_post, mem_norm, w_xq, w_xk, w_xv, w_xo,
           ln_ffn_pre, ln_ffn_post, w_up, conv_w, conv_b, w_down):
    B, S, D = x.shape
    M = mem.shape[1]
    depth = w_in.shape[0]
    N = B * S
    tm = min(ROW_TILE, S)
    blk = min(ATT_BLK, S)
    bq = min(ATT_QBLK, S)
    lc = min(HG_BLK, S)
    assert S % tm == 0 and S % blk == 0 and S % bq == 0 and bq % blk == 0 and blk % CHUNK == 0
    assert S % lc == 0 and lc % HG_SUB == 0
    tiles_per_batch = S // tm
    d_ff = w_down.shape[1]
    tf = FF_TILE
    tmf = min(FF_ROWS, S)
    assert d_ff % tf == 0 and S % tmf == 0 and tmf % FF_CHUNK == 0
    n_mla_in = MLA_Q_RANK + MLA_KV_RANK + MLA_ROPE

    pos = jnp.arange(S, dtype=F32)
    inv_freq = 1.0 / (ROPE_THETA ** (jnp.arange(0, MLA_ROPE, 2, dtype=F32) / MLA_ROPE))
    ang = pos[:, None] * inv_freq[None, :]
    cos, sin = jnp.cos(ang), jnp.sin(ang)
    cosT, sinT = cos.T, sin.T

    row = lambda v: v.reshape(1, -1).astype(F32)
    h = x
    for l in range(depth):
        w_mla = w_in[l][:, :n_mla_in].astype(BF16)
        w_hg = w_in[l][:, n_mla_in:].astype(BF16)
        w_uqT = w_uq[l].T.astype(BF16)
        w_ukv_h = w_ukv[l].reshape(MLA_KV_RANK, MLA_HEADS, MLA_NOPE + MLA_V)
        w_uk = w_ukv_h[:, :, :MLA_NOPE].reshape(MLA_KV_RANK, MLA_HEADS * MLA_NOPE).astype(BF16)
        w_uvT = w_ukv_h[:, :, MLA_NOPE:].reshape(MLA_KV_RANK, MLA_HEADS * MLA_V).T.astype(BF16)
        w_out_a = w_out[l][:MLA_WIDTH].astype(BF16)
        w_out_r = w_out[l][MLA_WIDTH:].astype(BF16)
        w_xkv = jnp.concatenate([w_xk[l], w_xv[l]], axis=1).astype(BF16)
        w_up_g = w_up[l][:, :d_ff].astype(BF16)
        w_up_v = w_up[l][:, d_ff:].astype(BF16)

        qT, k, vT = _mla_prep(h, row(ln_mix_pre[l]), w_mla, row(q_norm[l]), w_uqT, row(kv_norm[l]),
                              w_uk, w_uvT, cos, sin, cosT, sinT, blk)
        a = _attention(qT, k, vT, bq, blk)
        zh = _norm_matmul(h.reshape(N, D), row(ln_mix_pre[l]), w_hg, F32, tm, 1024, "hgrn_in")
        r = _hgrn(zh.reshape(B, S, -1), hgrn_lb.astype(F32), row(hgrn_out_norm[l]), l, lc)
        h2d = _mix_out(a.reshape(N, -1), r.reshape(N, -1), h.reshape(N, D), row(mla_out_norm[l]),
                       w_out_a, w_out_r, row(ln_mix_post[l]), tm)

        kvm = _norm_matmul(mem.reshape(B * M, D), row(mem_norm[l]), w_xkv, BF16, min(256, B * M), 1024, "mem_kv")
        h2d = _xattn(h2d, row(ln_x_pre[l]), w_xq[l].astype(BF16), kvm.reshape(B, M, 2 * D),
                     w_xo[l].astype(BF16), row(ln_x_post[l]), tm, tiles_per_batch)

        act = _ffn_up(h2d, row(ln_ffn_pre[l]), w_up_g, w_up_v, conv_w[l][:, :d_ff], conv_w[l][:, d_ff:],
                      row(conv_b[l][:d_ff]), row(conv_b[l][d_ff:]), tmf, tf, S // tmf)
        h2d = _ffn_down(act, h2d, w_down[l].astype(BF16), row(ln_ffn_post[l]), min(FF_DOWN_ROWS, S))
        h = h2d.reshape(B, S, D)
    return h
```

```python
import functools
import math

import jax
import jax.numpy as jnp
from jax import lax
from jax.experimental import pallas as pl
from jax.experimental.pallas import tpu as pltpu

F32 = jnp.float32
BF16 = jnp.bfloat16

EPS = 1e-6
CHUNK = 64
MLA_V = 128
MLA_NOPE = 128
MLA_ROPE = 64
MLA_HEADS = 8
MLA_QK = MLA_NOPE + MLA_ROPE
MLA_Q_RANK = 512
MLA_KV_RANK = 256
ROPE_THETA = 10000.0
HG_DK = 128
HG_DV = 128
HG_HEADS = 8
HG_WIDTH = HG_HEADS * HG_DV
MLA_WIDTH = MLA_HEADS * MLA_V
X_HEADS = 4
CONV_WIDTH = 3

LOG2E = 1.4426950408889634
NEG = -1e30

VMEM_LIMIT_BYTES = 56 * 1024 * 1024

ROW_TILE = 512
ATT_BLK = 512
ATT_QBLK = 2048
ATT_MIN_MASS = 2.0 ** -80
HG_BLK = 256
HG_SUB = 16
FF_TILE = 512
FF_ROWS = 1024
FF_CHUNK = 64
FF_DOWN_ROWS = 256
CONV_PAD = 8


def _params(*sem):
    return pltpu.CompilerParams(dimension_semantics=sem, vmem_limit_bytes=VMEM_LIMIT_BYTES)


def _rms(xf, g):
    return xf * lax.rsqrt(jnp.mean(xf * xf, axis=-1, keepdims=True) + EPS) * g


def _const_spec(shape):
    nd = len(shape)
    return pl.BlockSpec(shape, lambda *_: (0,) * nd, pipeline_mode=pl.Buffered(1))


def _mla_prep_body(x_ref, g_ref, wm_ref, qn_ref, wuqT_ref, kvn_ref, wuk_ref, wuvT_ref,
                   cos_ref, sin_ref, cosT_ref, sinT_ref, qT_ref, k_ref, vT_ref, *, q_scale):
    xn = _rms(x_ref[0], g_ref[...]).astype(BF16)
    z = jnp.dot(xn, wm_ref[...], preferred_element_type=F32)
    cq = _rms(z[:, :MLA_Q_RANK], qn_ref[...]).astype(BF16)
    ckv = _rms(z[:, MLA_Q_RANK:MLA_Q_RANK + MLA_KV_RANK], kvn_ref[...]).astype(BF16)
    kr = z[:, MLA_Q_RANK + MLA_KV_RANK:]

    nt = (((1,), (1,)), ((), ()))
    qT = lax.dot_general(wuqT_ref[...], cq, nt, preferred_element_type=F32)
    cosT = cosT_ref[...]
    sinT = sinT_ref[...]
    half = MLA_ROPE // 2
    for h in range(MLA_HEADS):
        base = h * MLA_QK
        x1 = qT[base + MLA_NOPE:base + MLA_NOPE + half]
        x2 = qT[base + MLA_NOPE + half:base + MLA_QK]
        qT_ref[0, base:base + MLA_NOPE, :] = (qT[base:base + MLA_NOPE] * q_scale).astype(BF16)
        qT_ref[0, base + MLA_NOPE:base + MLA_NOPE + half, :] = ((x1 * cosT - x2 * sinT) * q_scale).astype(BF16)
        qT_ref[0, base + MLA_NOPE + half:base + MLA_QK, :] = ((x2 * cosT + x1 * sinT) * q_scale).astype(BF16)

    kn = jnp.dot(ckv, wuk_ref[...], preferred_element_type=F32)
    cos = cos_ref[...]
    sin = sin_ref[...]
    k1 = kr[:, :half]
    k2 = kr[:, half:]
    krope = jnp.concatenate([k1 * cos - k2 * sin, k2 * cos + k1 * sin], axis=-1).astype(BF16)
    for h in range(MLA_HEADS):
        k_ref[0, h, :, 0:MLA_NOPE] = kn[:, h * MLA_NOPE:(h + 1) * MLA_NOPE].astype(BF16)
        k_ref[0, h, :, MLA_NOPE:MLA_QK] = krope

    vT = lax.dot_general(wuvT_ref[...], ckv, nt, preferred_element_type=F32)
    for h in range(MLA_HEADS):
        vT_ref[0, h, 0] = vT[h * MLA_V:(h + 1) * MLA_V].astype(BF16)


def _mla_prep(x, g, wm, qn, wuqT, kvn, wuk, wuvT, cos, sin, cosT, sinT, blk):
    B, S, D = x.shape
    nb = S // blk
    half = MLA_ROPE // 2
    q_scale = (MLA_QK ** -0.5) * LOG2E
    return pl.pallas_call(
        functools.partial(_mla_prep_body, q_scale=q_scale),
        grid=(B, nb),
        in_specs=[
            pl.BlockSpec((1, blk, D), lambda b, i: (b, i, 0)),
            _const_spec(g.shape), _const_spec(wm.shape), _const_spec(qn.shape), _const_spec(wuqT.shape),
            _const_spec(kvn.shape), _const_spec(wuk.shape), _const_spec(wuvT.shape),
            pl.BlockSpec((blk, half), lambda b, i: (i, 0)),
            pl.BlockSpec((blk, half), lambda b, i: (i, 0)),
            pl.BlockSpec((half, blk), lambda b, i: (0, i)),
            pl.BlockSpec((half, blk), lambda b, i: (0, i)),
        ],
        out_specs=[
            pl.BlockSpec((1, MLA_HEADS * MLA_QK, blk), lambda b, i: (b, 0, i)),
            pl.BlockSpec((1, MLA_HEADS, blk, MLA_QK), lambda b, i: (b, 0, i, 0)),
            pl.BlockSpec((1, MLA_HEADS, 1, MLA_V, blk), lambda b, i: (b, 0, i, 0, 0)),
        ],
        out_shape=[
            jax.ShapeDtypeStruct((B, MLA_HEADS * MLA_QK, S), BF16),
            jax.ShapeDtypeStruct((B, MLA_HEADS, S, MLA_QK), BF16),
            jax.ShapeDtypeStruct((B, MLA_HEADS, nb, MLA_V, blk), BF16),
        ],
        compiler_params=_params("parallel", "parallel"),
        name="mla_prep",
    )(x, g, wm, qn, wuqT, kvn, wuk, wuvT, cos, sin, cosT, sinT)


def _attn_body(qT_ref, k_ref, vT_ref, o_ref, kn_ref, m_ref, l_ref, acc_ref, *, bq, bk):
    i = pl.program_id(2)
    ratio = bq // bk
    qT = qT_ref[0]

    @pl.when(i == 0)
    def _():
        def kmax(c, best):
            kc = k_ref[0, 0, pl.ds(pl.multiple_of(c * bk, bk), bk), :].astype(F32)
            sq = jnp.sum(kc * kc, axis=-1, keepdims=True)
            return jnp.maximum(best, jnp.max(sq, axis=0, keepdims=True))

        best = lax.fori_loop(0, k_ref.shape[2] // bk, kmax, jnp.zeros((1, 1), F32))
        kn_ref[...] = jnp.broadcast_to(jnp.sqrt(best), kn_ref.shape)

    def scores(j, c0, masked):
        kblk = k_ref[0, 0, pl.ds(pl.multiple_of(j * bk, bk), bk), :]
        s = jnp.dot(kblk, qT[:, c0:], preferred_element_type=F32)
        if masked:
            key_chunk = lax.broadcasted_iota(jnp.int32, s.shape, 0) // CHUNK
            qry_chunk = lax.broadcasted_iota(jnp.int32, s.shape, 1) // CHUNK
            s = jnp.where(key_chunk <= qry_chunk, s, NEG)
        return s

    def sweep(update):
        l_ref[...] = jnp.zeros(l_ref.shape, F32)
        acc_ref[...] = jnp.zeros(acc_ref.shape, F32)

        def body(j, carry):
            update(j, 0, False)
            return carry

        lax.fori_loop(0, i * ratio, body, 0)
        for d in range(ratio):
            update(i * ratio + d, d * bk, True)

    def fixed_update(j, c0, masked):
        p = jnp.exp2(scores(j, c0, masked) - m_ref[:, c0:])
        l_ref[:, c0:] += jnp.sum(p, axis=0, keepdims=True)
        acc_ref[:, c0:] += jnp.dot(vT_ref[0, 0, j], p.astype(BF16), preferred_element_type=F32)

    qf = qT.astype(F32)
    m_ref[...] = jnp.sqrt(jnp.sum(qf * qf, axis=0, keepdims=True)) * kn_ref[...] + 1.0
    sweep(fixed_update)

    def online_update(j, c0, masked):
        s = scores(j, c0, masked)
        m_prev = m_ref[:, c0:]
        m_new = jnp.maximum(m_prev, jnp.max(s, axis=0, keepdims=True))
        alpha = jnp.exp2(m_prev - m_new)
        p = jnp.exp2(s - m_new)
        l_ref[:, c0:] = alpha * l_ref[:, c0:] + jnp.sum(p, axis=0, keepdims=True)
        acc_ref[:, c0:] = alpha * acc_ref[:, c0:] + jnp.dot(vT_ref[0, 0, j], p.astype(BF16),
                                                            preferred_element_type=F32)
        m_ref[:, c0:] = m_new

    @pl.when(jnp.logical_not(jnp.min(l_ref[...]) >= ATT_MIN_MASS))
    def _():
        m_ref[...] = jnp.full(m_ref.shape, NEG, F32)
        sweep(online_update)

    o_ref[0] = (acc_ref[...] / l_ref[...]).T


def _attention(qT, k, vT, bq, bk):
    B, H, S, _ = k.shape
    return pl.pallas_call(
        functools.partial(_attn_body, bq=bq, bk=bk),
        grid=(B, H, S // bq),
        in_specs=[
            pl.BlockSpec((1, MLA_QK, bq), lambda b, h, i: (b, h, i)),
            pl.BlockSpec((1, 1, S, MLA_QK), lambda b, h, i: (b, h, 0, 0)),
            pl.BlockSpec((1, 1, S // bk, MLA_V, bk), lambda b, h, i: (b, h, 0, 0, 0)),
        ],
        out_specs=pl.BlockSpec((1, bq, MLA_V), lambda b, h, i: (b, i, h)),
        out_shape=jax.ShapeDtypeStruct((B, S, H * MLA_V), F32),
        scratch_shapes=[pltpu.VMEM((1, bq), F32), pltpu.VMEM((1, bq), F32), pltpu.VMEM((1, bq), F32),
                        pltpu.VMEM((MLA_V, bq), F32)],
        compiler_params=_params("parallel", "parallel", "arbitrary"),
        name="mla_attention",
    )(qT, k, vT)


def _norm_matmul_body(x_ref, g_ref, w_ref, o_ref, xn_ref):
    @pl.when(pl.program_id(1) == 0)
    def _():
        xn_ref[...] = _rms(x_ref[...], g_ref[...]).astype(BF16)

    o_ref[...] = jnp.dot(xn_ref[...], w_ref[...], preferred_element_type=F32).astype(o_ref.dtype)


def _norm_matmul(x2d, g, w, out_dtype, tm, tn, name):
    N, D = x2d.shape
    M = w.shape[1]
    return pl.pallas_call(
        _norm_matmul_body,
        grid=(N // tm, M // tn),
        in_specs=[
            pl.BlockSpec((tm, D), lambda i, j: (i, 0)),
            _const_spec(g.shape),
            pl.BlockSpec((D, tn), lambda i, j: (0, j)),
        ],
        out_specs=pl.BlockSpec((tm, tn), lambda i, j: (i, j)),
        out_shape=jax.ShapeDtypeStruct((N, M), out_dtype),
        scratch_shapes=[pltpu.VMEM((tm, D), BF16)],
        compiler_params=_params("parallel", "arbitrary"),
        name=name,
    )(x2d, g, w)


def _group_scan(x, row_in_group, group, reverse):
    n = x.shape[0]
    sh = 1
    while sh < group:
        if reverse:
            x = x + jnp.where(row_in_group < group - sh, pltpu.roll(x, n - sh, axis=0), 0.0)
        else:
            x = x + jnp.where(row_in_group >= sh, pltpu.roll(x, sh, axis=0), 0.0)
        sh *= 2
    return x


def _hgrn_body(hq_ref, hf_ref, hi_ref, hg_ref, lbp_ref, gain_ref, o_ref,
               st_ref, q_s, k_s, b_s, v_s, qd_s, kd_s, es_s, o_s, *, lc, layer):
    T = HG_SUB
    H = HG_HEADS
    half = T // 2

    @pl.when(pl.program_id(1) == 0)
    def _():
        st_ref[...] = jnp.zeros(st_ref.shape, F32)

    lbp = lbp_ref[...]
    e = jnp.exp(lbp - jnp.max(lbp, axis=0, keepdims=True))
    prob = e / jnp.sum(e, axis=0, keepdims=True)
    lb = jnp.sum(prob[:layer + 1], axis=0, keepdims=True)

    hq = hq_ref[0]
    q = hq * jax.nn.sigmoid(hq)
    f = lb + (1.0 - lb) * jax.nn.sigmoid(hf_ref[0])
    k = 1.0 - f
    logf = jnp.log(f)

    rig = lax.broadcasted_iota(jnp.int32, logf.shape, 0) % T
    b = _group_scan(logf, rig, T, reverse=False)
    suffix = _group_scan(logf, rig, T, reverse=True)
    staged = ((q_s, q), (k_s, k), (b_s, b), (v_s, hi_ref[0]), (qd_s, q * jnp.exp(b)),
              (kd_s, k * jnp.exp(suffix - logf)),
              (es_s, jnp.exp(suffix)))
    for ref, val in staged:
        for h in range(H):
            ref[h] = val[:, h * HG_DK:(h + 1) * HG_DK]

    row = lax.broadcasted_iota(jnp.int32, (half, HG_DK), 0)
    nt = (((1,), (1,)), ((), ()))
    tn = (((0,), (0,)), ((), ()))

    def sub(i, carry):
        r0 = pl.multiple_of(i * T, T)
        r1 = pl.multiple_of(i * T + half, half)
        for h in range(H):
            cs = slice(h * HG_DK, (h + 1) * HG_DK)
            q_lo = q_s[h, pl.ds(r0, half), :]
            q_hi = q_s[h, pl.ds(r1, half), :]
            b_lo = b_s[h, pl.ds(r0, half), :]
            b_hi = b_s[h, pl.ds(r1, half), :]
            acc_lo = jnp.zeros((half, HG_DV), F32)
            acc_hi = jnp.zeros((half, HG_DV), F32)
            for s in range(T):
                ks = k_s[h, pl.ds(r0 + s, 1), :]
                bs = b_s[h, pl.ds(r0 + s, 1), :]
                vs = v_s[h, pl.ds(r0 + s, 1), :]
                if s < half:
                    d_lo = jnp.exp(jnp.where(row >= s, b_lo - bs, NEG))
                    acc_lo = acc_lo + jnp.sum(q_lo * ks * d_lo, axis=-1, keepdims=True) * vs
                    d_hi = jnp.exp(b_hi - bs)
                else:
                    d_hi = jnp.exp(jnp.where(row >= s - half, b_hi - bs, NEG))
                acc_hi = acc_hi + jnp.sum(q_hi * ks * d_hi, axis=-1, keepdims=True) * vs
            st = st_ref[h]
            inter = lax.dot_general(qd_s[h, pl.ds(r0, T), :], st, nt, preferred_element_type=F32)
            o_s[h, pl.ds(r0, half), :] = acc_lo + inter[:half]
            o_s[h, pl.ds(r1, half), :] = acc_hi + inter[half:]
            upd = lax.dot_general(v_s[h, pl.ds(r0, T), :], kd_s[h, pl.ds(r0, T), :], tn,
                                  preferred_element_type=F32)
            st_ref[h] = st * es_s[h, pl.ds(r0, 1), :] + upd
        return carry

    lax.fori_loop(0, lc // T, sub, 0)

    gain = gain_ref[...]
    for h in range(H):
        cs = slice(h * HG_DV, (h + 1) * HG_DV)
        hg = hg_ref[0, :, cs]
        o_ref[0, :, cs] = _rms(o_s[h], gain) * (hg * jax.nn.sigmoid(hg))


def _hgrn(zh, lb_table, gain, layer, lc):
    B, S, _ = zh.shape
    W = HG_HEADS * HG_DK
    blk = lambda part: pl.BlockSpec((1, lc, W), lambda b, c: (b, c, part))
    scr = lambda: pltpu.VMEM((HG_HEADS, lc, HG_DK), F32)
    return pl.pallas_call(
        functools.partial(_hgrn_body, lc=lc, layer=layer),
        grid=(B, S // lc),
        in_specs=[
            blk(0), blk(1), blk(2), blk(3),
            pl.BlockSpec(lb_table.shape, lambda b, c: (0, 0)),
            pl.BlockSpec((1, HG_DV), lambda b, c: (0, 0)),
        ],
        out_specs=pl.BlockSpec((1, lc, W), lambda b, c: (b, c, 0)),
        out_shape=jax.ShapeDtypeStruct((B, S, W), F32),
        scratch_shapes=[pltpu.VMEM((HG_HEADS, HG_DV, HG_DK), F32)] + [scr() for _ in range(8)],
        compiler_params=_params("parallel", "arbitrary"),
        name="hgrn2",
    )(zh, zh, zh, zh, lb_table, gain)


def _mix_out_body(a_ref, r_ref, x_ref, gm_ref, wa_ref, wr_ref, gp_ref, o_ref):
    an = _rms(a_ref[...], gm_ref[...]).astype(BF16)
    y = jnp.dot(an, wa_ref[...], preferred_element_type=F32)
    y = y + jnp.dot(r_ref[...].astype(BF16), wr_ref[...], preferred_element_type=F32)
    o_ref[...] = x_ref[...] + _rms(y, gp_ref[...])


def _mix_out(a2d, r2d, x2d, gm, wa, wr, gp, tm):
    N, D = x2d.shape
    return pl.pallas_call(
        _mix_out_body,
        grid=(N // tm,),
        in_specs=[
            pl.BlockSpec((tm, a2d.shape[1]), lambda i: (i, 0)),
            pl.BlockSpec((tm, r2d.shape[1]), lambda i: (i, 0)),
            pl.BlockSpec((tm, D), lambda i: (i, 0)),
            _const_spec(gm.shape), _const_spec(wa.shape), _const_spec(wr.shape), _const_spec(gp.shape),
        ],
        out_specs=pl.BlockSpec((tm, D), lambda i: (i, 0)),
        out_shape=jax.ShapeDtypeStruct((N, D), F32),
        compiler_params=_params("parallel"),
        name="mix_out",
    )(a2d, r2d, x2d, gm, wa, wr, gp)


def _xattn_body(h_ref, gpre_ref, wq_ref, kv_ref, wo_ref, gpost_ref, o_ref, oh_ref, *, q_scale):
    h = h_ref[...]
    D = h.shape[-1]
    dh = D // X_HEADS
    xn = _rms(h, gpre_ref[...]).astype(BF16)
    qx = (jnp.dot(xn, wq_ref[...], preferred_element_type=F32) * q_scale).astype(BF16)
    for hd in range(X_HEADS):
        qh = qx[:, hd * dh:(hd + 1) * dh]
        kh = kv_ref[0, :, hd * dh:(hd + 1) * dh]
        vh = kv_ref[0, :, D + hd * dh:D + (hd + 1) * dh]
        s = lax.dot_general(qh, kh, (((1,), (1,)), ((), ())), preferred_element_type=F32)
        p = jnp.exp2(s - jnp.max(s, axis=-1, keepdims=True))
        l = jnp.sum(p, axis=-1, keepdims=True)
        oh = jnp.dot(p.astype(BF16), vh, preferred_element_type=F32) / l
        oh_ref[:, hd * dh:(hd + 1) * dh] = oh.astype(BF16)
    ox = jnp.dot(oh_ref[...], wo_ref[...], preferred_element_type=F32)
    o_ref[...] = h + _rms(ox, gpost_ref[...])


def _xattn(h2d, gpre, wq, kvm, wo, gpost, tm, tiles_per_batch):
    N, D = h2d.shape
    M = kvm.shape[1]
    q_scale = ((D // X_HEADS) ** -0.5) * LOG2E
    return pl.pallas_call(
        functools.partial(_xattn_body, q_scale=q_scale),
        grid=(N // tm,),
        in_specs=[
            pl.BlockSpec((tm, D), lambda i: (i, 0)),
            _const_spec(gpre.shape), _const_spec(wq.shape),
            pl.BlockSpec((1, M, 2 * D), lambda i: (i // tiles_per_batch, 0, 0)),
            _const_spec(wo.shape), _const_spec(gpost.shape),
        ],
        out_specs=pl.BlockSpec((tm, D), lambda i: (i, 0)),
        out_shape=jax.ShapeDtypeStruct((N, D), F32),
        scratch_shapes=[pltpu.VMEM((tm, D), BF16)],
        compiler_params=_params("parallel"),
        name="xattn",
    )(h2d, gpre, wq, kvm, wo, gpost)


def _gelu_tanh(x):
    return x * (0.5 * (1.0 + jnp.tanh(math.sqrt(2.0 / math.pi) * (x + 0.044715 * (x * x * x)))))


def _ffn_up_body(h_ref, gpre_ref, wg_ref, wv_ref, cwg_ref, cwv_ref, cbg_ref, cbv_ref,
                 act_ref, xn_ref, cg_ref, cv_ref, bg0_ref, bv0_ref, bg1_ref, bv1_ref,
                 *, tm, nj, tiles_per_batch):
    i = pl.program_id(0)
    j = pl.program_id(1)
    first = (i % tiles_per_batch) == 0
    slots = ((bg0_ref, bv0_ref), (bg1_ref, bv1_ref))

    def project(slot):
        for w_ref, carry_ref, buf_ref in zip((wg_ref, wv_ref), (cg_ref, cv_ref), slots[slot]):
            u = jnp.dot(xn_ref[...], w_ref[...], preferred_element_type=F32)
            buf_ref[0:CONV_PAD, :] = jnp.where(first, 0.0, carry_ref[j])
            buf_ref[CONV_PAD:CONV_PAD + tm, :] = u
            carry_ref[j] = u[tm - CONV_PAD:, :]

    def conv(buf_ref, cw, cb, r):
        base = CONV_PAD + r
        return (cw[0:1, :] * buf_ref[base - 2:base - 2 + FF_CHUNK, :]
                + cw[1:2, :] * buf_ref[base - 1:base - 1 + FF_CHUNK, :]
                + cw[2:3, :] * buf_ref[base:base + FF_CHUNK, :] + cb)

    def finish(slot):
        bg_ref, bv_ref = slots[slot]
        cwg = cwg_ref[...]
        cwv = cwv_ref[...]
        cbg = cbg_ref[...]
        cbv = cbv_ref[...]
        for r in range(0, tm, FF_CHUNK):
            gate = conv(bg_ref, cwg, cbg, r)
            val = conv(bv_ref, cwv, cbv, r)
            act_ref[r:r + FF_CHUNK, :] = (_gelu_tanh(gate) * val).astype(BF16)

    @pl.when(j == 0)
    def _():
        @pl.when(i == 0)
        def _():
            cg_ref[...] = jnp.zeros(cg_ref.shape, F32)
            cv_ref[...] = jnp.zeros(cv_ref.shape, F32)

        xn_ref[...] = _rms(h_ref[...], gpre_ref[...]).astype(BF16)
        project(0)

    for parity in (0, 1):
        @pl.when((j > 0) & (j < nj) & (j % 2 == parity))
        def _():
            finish(1 - parity)
            project(parity)

    @pl.when(j == nj)
    def _():
        finish((nj - 1) % 2)


def _ffn_up(h2d, gpre, wg, wv, cwg, cwv, cbg, cbv, tm, tf, tiles_per_batch):
    N, D = h2d.shape
    F = wg.shape[1]
    nj = F // tf
    cur = lambda i, j: (0, jnp.minimum(j, nj - 1))
    prev = lambda i, j: (0, jnp.maximum(j - 1, 0))
    buf = lambda: pltpu.VMEM((CONV_PAD + tm, tf), F32)
    return pl.pallas_call(
        functools.partial(_ffn_up_body, tm=tm, nj=nj, tiles_per_batch=tiles_per_batch),
        grid=(N // tm, nj + 1),
        in_specs=[
            pl.BlockSpec((tm, D), lambda i, j: (i, 0)),
            _const_spec(gpre.shape),
            pl.BlockSpec((D, tf), cur),
            pl.BlockSpec((D, tf), cur),
            pl.BlockSpec((CONV_WIDTH, tf), prev), pl.BlockSpec((CONV_WIDTH, tf), prev),
            pl.BlockSpec((1, tf), prev), pl.BlockSpec((1, tf), prev),
        ],
        out_specs=pl.BlockSpec((tm, tf), lambda i, j: (i, jnp.maximum(j - 1, 0))),
        out_shape=jax.ShapeDtypeStruct((N, F), BF16),
        scratch_shapes=[
            pltpu.VMEM((tm, D), BF16),
            pltpu.VMEM((nj, CONV_PAD, tf), F32),
            pltpu.VMEM((nj, CONV_PAD, tf), F32),
            buf(), buf(), buf(), buf(),
        ],
        compiler_params=_params("arbitrary", "arbitrary"),
        name="ffn_up",
    )(h2d, gpre, wg, wv, cwg, cwv, cbg, cbv)


def _ffn_down_body(act_ref, h_ref, wd_ref, gpost_ref, o_ref):
    y = jnp.dot(act_ref[...], wd_ref[...], preferred_element_type=F32)
    o_ref[...] = h_ref[...] + _rms(y, gpost_ref[...])


def _ffn_down(act, h2d, wd, gpost, tm):
    N, D = h2d.shape
    F = act.shape[1]
    return pl.pallas_call(
        _ffn_down_body,
        grid=(N // tm,),
        in_specs=[
            pl.BlockSpec((tm, F), lambda i: (i, 0)),
            pl.BlockSpec((tm, D), lambda i: (i, 0)),
            _const_spec(wd.shape), _const_spec(gpost.shape),
        ],
        out_specs=pl.BlockSpec((tm, D), lambda i: (i, 0)),
        out_shape=jax.ShapeDtypeStruct((N, D), F32),
        compiler_params=_params("parallel"),
        name="ffn_down",
    )(act, h2d, wd, gpost)


def kernel(x, mem, w_in, q_norm, w_uq, kv_norm, w_ukv, mla_out_norm, hgrn_lb, hgrn_out_norm, w_out,
           ln_mix_pre, ln_mix_post, ln_x_pre, ln_x_post, mem_norm, w_xq, w_xk, w_xv, w_xo,
           ln_ffn_pre, ln_ffn_post, w_up, conv_w, conv_b, w_down):
    B, S, D = x.shape
    M = mem.shape[1]
    depth = w_in.shape[0]
    N = B * S
    tm = min(ROW_TILE, S)
    blk = min(ATT_BLK, S)
    bq = min(ATT_QBLK, S)
    lc = min(HG_BLK, S)
    assert S % tm == 0 and S % blk == 0 and S % bq == 0 and bq % blk == 0 and blk % CHUNK == 0
    assert S % lc == 0 and lc % HG_SUB == 0
    tiles_per_batch = S // tm
    d_ff = w_down.shape[1]
    tf = FF_TILE
    tmf = min(FF_ROWS, S)
    assert d_ff % tf == 0 and S % tmf == 0 and tmf % FF_CHUNK == 0
    n_mla_in = MLA_Q_RANK + MLA_KV_RANK + MLA_ROPE

    pos = jnp.arange(S, dtype=F32)
    inv_freq = 1.0 / (ROPE_THETA ** (jnp.arange(0, MLA_ROPE, 2, dtype=F32) / MLA_ROPE))
    ang = pos[:, None] * inv_freq[None, :]
    cos, sin = jnp.cos(ang), jnp.sin(ang)
    cosT, sinT = cos.T, sin.T

    row = lambda v: v.reshape(1, -1).astype(F32)
    h = x
    for l in range(depth):
        w_mla = w_in[l][:, :n_mla_in].astype(BF16)
        w_hg = w_in[l][:, n_mla_in:].astype(BF16)
        w_uqT = w_uq[l].T.astype(BF16)
        w_ukv_h = w_ukv[l].reshape(MLA_KV_RANK, MLA_HEADS, MLA_NOPE + MLA_V)
        w_uk = w_ukv_h[:, :, :MLA_NOPE].reshape(MLA_KV_RANK, MLA_HEADS * MLA_NOPE).astype(BF16)
        w_uvT = w_ukv_h[:, :, MLA_NOPE:].reshape(MLA_KV_RANK, MLA_HEADS * MLA_V).T.astype(BF16)
        w_out_a = w_out[l][:MLA_WIDTH].astype(BF16)
        w_out_r = w_out[l][MLA_WIDTH:].astype(BF16)
        w_xkv = jnp.concatenate([w_xk[l], w_xv[l]], axis=1).astype(BF16)
        w_up_g = w_up[l][:, :d_ff].astype(BF16)
        w_up_v = w_up[l][:, d_ff:].astype(BF16)

        qT, k, vT = _mla_prep(h, row(ln_mix_pre[l]), w_mla, row(q_norm[l]), w_uqT, row(kv_norm[l]),
                              w_uk, w_uvT, cos, sin, cosT, sinT, blk)
        a = _attention(qT, k, vT, bq, blk)
        zh = _norm_matmul(h.reshape(N, D), row(ln_mix_pre[l]), w_hg, F32, tm, 1024, "hgrn_in")
        r = _hgrn(zh.reshape(B, S, -1), hgrn_lb.astype(F32), row(hgrn_out_norm[l]), l, lc)
        h2d = _mix_out(a.reshape(N, -1), r.reshape(N, -1), h.reshape(N, D), row(mla_out_norm[l]),
                       w_out_a, w_out_r, row(ln_mix_post[l]), tm)

        kvm = _norm_matmul(mem.reshape(B * M, D), row(mem_norm[l]), w_xkv, BF16, min(256, B * M), 1024, "mem_kv")
        h2d = _xattn(h2d, row(ln_x_pre[l]), w_xq[l].astype(BF16), kvm.reshape(B, M, 2 * D),
                     w_xo[l].astype(BF16), row(ln_x_post[l]), tm, tiles_per_batch)

        act = _ffn_up(h2d, row(ln_ffn_pre[l]), w_up_g, w_up_v, conv_w[l][:, :d_ff], conv_w[l][:, d_ff:],
                      row(conv_b[l][:d_ff]), row(conv_b[l][d_ff:]), tmf, tf, S // tmf)
        h2d = _ffn_down(act, h2d, w_down[l].astype(BF16), row(ln_ffn_post[l]), min(FF_DOWN_ROWS, S))
        h = h2d.reshape(B, S, D)
    return h
```

```python
import functools
import math

import jax
import jax.numpy as jnp
from jax import lax
from jax.experimental import pallas as pl
from jax.experimental.pallas import tpu as pltpu

F32 = jnp.float32
BF16 = jnp.bfloat16

EPS = 1e-6
CHUNK = 64
MLA_V = 128
MLA_NOPE = 128
MLA_ROPE = 64
MLA_HEADS = 8
MLA_QK = MLA_NOPE + MLA_ROPE
MLA_Q_RANK = 512
MLA_KV_RANK = 256
ROPE_THETA = 10000.0
HG_DK = 128
HG_DV = 128
HG_HEADS = 8
HG_WIDTH = HG_HEADS * HG_DV
MLA_WIDTH = MLA_HEADS * MLA_V
X_HEADS = 4
CONV_WIDTH = 3

LOG2E = 1.4426950408889634
NEG = -1e30

VMEM_LIMIT_BYTES = 56 * 1024 * 1024

ROW_TILE = 512
ATT_BLK = 512
ATT_QBLK = 2048
ATT_MIN_MASS = 2.0 ** -80
HG_BLK = 256
HG_SUB = 16
FF_TILE = 512
FF_ROWS = 1024
FF_CHUNK = 64
FF_DOWN_ROWS = 256
CONV_PAD = 8


def _params(*sem):
    return pltpu.CompilerParams(dimension_semantics=sem, vmem_limit_bytes=VMEM_LIMIT_BYTES)


def _rms(xf, g):
    return xf * lax.rsqrt(jnp.mean(xf * xf, axis=-1, keepdims=True) + EPS) * g


def _const_spec(shape):
    nd = len(shape)
    return pl.BlockSpec(shape, lambda *_: (0,) * nd, pipeline_mode=pl.Buffered(1))


def _mla_prep_body(x_ref, g_ref, wm_ref, qn_ref, wuqT_ref, kvn_ref, wuk_ref, wuvT_ref,
                   cos_ref, sin_ref, cosT_ref, sinT_ref, qT_ref, k_ref, vT_ref, *, q_scale):
    xn = _rms(x_ref[0], g_ref[...]).astype(BF16)
    z = jnp.dot(xn, wm_ref[...], preferred_element_type=F32)
    cq = _rms(z[:, :MLA_Q_RANK], qn_ref[...]).astype(BF16)
    ckv = _rms(z[:, MLA_Q_RANK:MLA_Q_RANK + MLA_KV_RANK], kvn_ref[...]).astype(BF16)
    kr = z[:, MLA_Q_RANK + MLA_KV_RANK:]

    nt = (((1,), (1,)), ((), ()))
    qT = lax.dot_general(wuqT_ref[...], cq, nt, preferred_element_type=F32)
    cosT = cosT_ref[...]
    sinT = sinT_ref[...]
    half = MLA_ROPE // 2
    for h in range(MLA_HEADS):
        base = h * MLA_QK
        x1 = qT[base + MLA_NOPE:base + MLA_NOPE + half]
        x2 = qT[base + MLA_NOPE + half:base + MLA_QK]
        qT_ref[0, base:base + MLA_NOPE, :] = (qT[base:base + MLA_NOPE] * q_scale).astype(BF16)
        qT_ref[0, base + MLA_NOPE:base + MLA_NOPE + half, :] = ((x1 * cosT - x2 * sinT) * q_scale).astype(BF16)
        qT_ref[0, base + MLA_NOPE + half:base + MLA_QK, :] = ((x2 * cosT + x1 * sinT) * q_scale).astype(BF16)

    kn = jnp.dot(ckv, wuk_ref[...], preferred_element_type=F32)
    cos = cos_ref[...]
    sin = sin_ref[...]
    k1 = kr[:, :half]
    k2 = kr[:, half:]
    krope = jnp.concatenate([k1 * cos - k2 * sin, k2 * cos + k1 * sin], axis=-1).astype(BF16)
    for h in range(MLA_HEADS):
        k_ref[0, h, :, 0:MLA_NOPE] = kn[:, h * MLA_NOPE:(h + 1) * MLA_NOPE].astype(BF16)
        k_ref[0, h, :, MLA_NOPE:MLA_QK] = krope

    vT = lax.dot_general(wuvT_ref[...], ckv, nt, preferred_element_type=F32)
    for h in range(MLA_HEADS):
        vT_ref[0, h, 0] = vT[h * MLA_V:(h + 1) * MLA_V].astype(BF16)


def _mla_prep(x, g, wm, qn, wuqT, kvn, wuk, wuvT, cos, sin, cosT, sinT, blk):
    B, S, D = x.shape
    nb = S // blk
    half = MLA_ROPE // 2
    q_scale = (MLA_QK ** -0.5) * LOG2E
    return pl.pallas_call(
        functools.partial(_mla_prep_body, q_scale=q_scale),
        grid=(B, nb),
        in_specs=[
            pl.BlockSpec((1, blk, D), lambda b, i: (b, i, 0)),
            _const_spec(g.shape), _const_spec(wm.shape), _const_spec(qn.shape), _const_spec(wuqT.shape),
            _const_spec(kvn.shape), _const_spec(wuk.shape), _const_spec(wuvT.shape),
            pl.BlockSpec((blk, half), lambda b, i: (i, 0)),
            pl.BlockSpec((blk, half), lambda b, i: (i, 0)),
            pl.BlockSpec((half, blk), lambda b, i: (0, i)),
            pl.BlockSpec((half, blk), lambda b, i: (0, i)),
        ],
        out_specs=[
            pl.BlockSpec((1, MLA_HEADS * MLA_QK, blk), lambda b, i: (b, 0, i)),
            pl.BlockSpec((1, MLA_HEADS, blk, MLA_QK), lambda b, i: (b, 0, i, 0)),
            pl.BlockSpec((1, MLA_HEADS, 1, MLA_V, blk), lambda b, i: (b, 0, i, 0, 0)),
        ],
        out_shape=[
            jax.ShapeDtypeStruct((B, MLA_HEADS * MLA_QK, S), BF16),
            jax.ShapeDtypeStruct((B, MLA_HEADS, S, MLA_QK), BF16),
            jax.ShapeDtypeStruct((B, MLA_HEADS, nb, MLA_V, blk), BF16),
        ],
        compiler_params=_params("parallel", "parallel"),
        name="mla_prep",
    )(x, g, wm, qn, wuqT, kvn, wuk, wuvT, cos, sin, cosT, sinT)


def _attn_body(qT_ref, k_ref, vT_ref, o_ref, kn_ref, m_ref, l_ref, acc_ref, *, bq, bk):
    i = pl.program_id(2)
    ratio = bq // bk
    qT = qT_ref[0]

    @pl.when(i == 0)
    def _():
        def kmax(c, best):
            kc = k_ref[0, 0, pl.ds(pl.multiple_of(c * bk, bk), bk), :].astype(F32)
            sq = jnp.sum(kc * kc, axis=-1, keepdims=True)
            return jnp.maximum(best, jnp.max(sq, axis=0, keepdims=True))

        best = lax.fori_loop(0, k_ref.shape[2] // bk, kmax, jnp.zeros((1, 1), F32))
        kn_ref[...] = jnp.broadcast_to(jnp.sqrt(best), kn_ref.shape)

    def scores(j, c0, masked):
        kblk = k_ref[0, 0, pl.ds(pl.multiple_of(j * bk, bk), bk), :]
        s = jnp.dot(kblk, qT[:, c0:], preferred_element_type=F32)
        if masked:
            key_chunk = lax.broadcasted_iota(jnp.int32, s.shape, 0) // CHUNK
            qry_chunk = lax.broadcasted_iota(jnp.int32, s.shape, 1) // CHUNK
            s = jnp.where(key_chunk <= qry_chunk, s, NEG)
        return s

    def sweep(update):
        l_ref[...] = jnp.zeros(l_ref.shape, F32)
        acc_ref[...] = jnp.zeros(acc_ref.shape, F32)

        def body(j, carry):
            update(j, 0, False)
            return carry

        lax.fori_loop(0, i * ratio, body, 0)
        for d in range(ratio):
            update(i * ratio + d, d * bk, True)

    def fixed_update(j, c0, masked):
        p = jnp.exp2(scores(j, c0, masked) - m_ref[:, c0:])
        l_ref[:, c0:] += jnp.sum(p, axis=0, keepdims=True)
        acc_ref[:, c0:] += jnp.dot(vT_ref[0, 0, j], p.astype(BF16), preferred_element_type=F32)

    qf = qT.astype(F32)
    m_ref[...] = jnp.sqrt(jnp.sum(qf * qf, axis=0, keepdims=True)) * kn_ref[...] + 1.0
    sweep(fixed_update)

    def online_update(j, c0, masked):
        s = scores(j, c0, masked)
        m_prev = m_ref[:, c0:]
        m_new = jnp.maximum(m_prev, jnp.max(s, axis=0, keepdims=True))
        alpha = jnp.exp2(m_prev - m_new)
        p = jnp.exp2(s - m_new)
        l_ref[:, c0:] = alpha * l_ref[:, c0:] + jnp.sum(p, axis=0, keepdims=True)
        acc_ref[:, c0:] = alpha * acc_ref[:, c0:] + jnp.dot(vT_ref[0, 0, j], p.astype(BF16),
                                                            preferred_element_type=F32)
        m_ref[:, c0:] = m_new

    @pl.when(jnp.logical_not(jnp.min(l_ref[...]) >= ATT_MIN_MASS))
    def _():
        m_ref[...] = jnp.full(m_ref.shape, NEG, F32)
        sweep(online_update)

    o_ref[0] = (acc_ref[...] / l_ref[...]).T


def _attention(qT, k, vT, bq, bk):
    B, H, S, _ = k.shape
    return pl.pallas_call(
        functools.partial(_attn_body, bq=bq, bk=bk),
        grid=(B, H, S // bq),
        in_specs=[
            pl.BlockSpec((1, MLA_QK, bq), lambda b, h, i: (b, h, i)),
            pl.BlockSpec((1, 1, S, MLA_QK), lambda b, h, i: (b, h, 0, 0)),
            pl.BlockSpec((1, 1, S // bk, MLA_V, bk), lambda b, h, i: (b, h, 0, 0, 0)),
        ],
        out_specs=pl.BlockSpec((1, bq, MLA_V), lambda b, h, i: (b, i, h)),
        out_shape=jax.ShapeDtypeStruct((B, S, H * MLA_V), F32),
        scratch_shapes=[pltpu.VMEM((1, bq), F32), pltpu.VMEM((1, bq), F32), pltpu.VMEM((1, bq), F32),
                        pltpu.VMEM((MLA_V, bq), F32)],
        compiler_params=_params("parallel", "parallel", "arbitrary"),
        name="mla_attention",
    )(qT, k, vT)


def _norm_matmul_body(x_ref, g_ref, w_ref, o_ref, xn_ref):
    @pl.when(pl.program_id(1) == 0)
    def _():
        xn_ref[...] = _rms(x_ref[...], g_ref[...]).astype(BF16)

    o_ref[...] = jnp.dot(xn_ref[...], w_ref[...], preferred_element_type=F32).astype(o_ref.dtype)


def _norm_matmul(x2d, g, w, out_dtype, tm, tn, name):
    N, D = x2d.shape
    M = w.shape[1]
    w_spec = _const_spec(w.shape) if tn == M else pl.BlockSpec((D, tn), lambda i, j: (0, j))
    return pl.pallas_call(
        _norm_matmul_body,
        grid=(N // tm, M // tn),
        in_specs=[
            pl.BlockSpec((tm, D), lambda i, j: (i, 0)),
            _const_spec(g.shape),
            w_spec,
        ],
        out_specs=pl.BlockSpec((tm, tn), lambda i, j: (i, j)),
        out_shape=jax.ShapeDtypeStruct((N, M), out_dtype),
        scratch_shapes=[pltpu.VMEM((tm, D), BF16)],
        compiler_params=_params("parallel", "arbitrary"),
        name=name,
    )(x2d, g, w)


def _group_scan(x, row_in_group, group, reverse):
    n = x.shape[0]
    sh = 1
    while sh < group:
        if reverse:
            x = x + jnp.where(row_in_group < group - sh, pltpu.roll(x, n - sh, axis=0), 0.0)
        else:
            x = x + jnp.where(row_in_group >= sh, pltpu.roll(x, sh, axis=0), 0.0)
        sh *= 2
    return x


def _hgrn_body(hq_ref, hf_ref, hi_ref, hg_ref, lbp_ref, gain_ref, o_ref,
               st_ref, q_s, k_s, b_s, v_s, qd_s, kd_s, es_s, o_s, *, lc, layer):
    T = HG_SUB
    H = HG_HEADS
    half = T // 2

    @pl.when(pl.program_id(1) == 0)
    def _():
        st_ref[...] = jnp.zeros(st_ref.shape, F32)

    lbp = lbp_ref[...]
    e = jnp.exp(lbp - jnp.max(lbp, axis=0, keepdims=True))
    prob = e / jnp.sum(e, axis=0, keepdims=True)
    lb = jnp.sum(prob[:layer + 1], axis=0, keepdims=True)

    hq = hq_ref[0]
    q = hq * jax.nn.sigmoid(hq)
    f = lb + (1.0 - lb) * jax.nn.sigmoid(hf_ref[0])
    k = 1.0 - f
    logf = jnp.log(f)

    rig = lax.broadcasted_iota(jnp.int32, logf.shape, 0) % T
    b = _group_scan(logf, rig, T, reverse=False)
    suffix = _group_scan(logf, rig, T, reverse=True)
    staged = ((q_s, q), (k_s, k), (b_s, b), (v_s, hi_ref[0]), (qd_s, q * jnp.exp(b)),
              (kd_s, k * jnp.exp(suffix - logf)),
              (es_s, jnp.exp(suffix)))
    for ref, val in staged:
        for h in range(H):
            ref[h] = val[:, h * HG_DK:(h + 1) * HG_DK]

    row = lax.broadcasted_iota(jnp.int32, (half, HG_DK), 0)
    nt = (((1,), (1,)), ((), ()))
    tn = (((0,), (0,)), ((), ()))

    def sub(i, carry):
        r0 = pl.multiple_of(i * T, T)
        r1 = pl.multiple_of(i * T + half, half)
        for h in range(H):
            cs = slice(h * HG_DK, (h + 1) * HG_DK)
            q_lo = q_s[h, pl.ds(r0, half), :]
            q_hi = q_s[h, pl.ds(r1, half), :]
            b_lo = b_s[h, pl.ds(r0, half), :]
            b_hi = b_s[h, pl.ds(r1, half), :]
            acc_lo = jnp.zeros((half, HG_DV), F32)
            acc_hi = jnp.zeros((half, HG_DV), F32)
            for s in range(T):
                ks = k_s[h, pl.ds(r0 + s, 1), :]
                bs = b_s[h, pl.ds(r0 + s, 1), :]
                vs = v_s[h, pl.ds(r0 + s, 1), :]
                if s < half:
                    d_lo = jnp.exp(jnp.where(row >= s, b_lo - bs, NEG))
                    acc_lo = acc_lo + jnp.sum(q_lo * ks * d_lo, axis=-1, keepdims=True) * vs
                    d_hi = jnp.exp(b_hi - bs)
                else:
                    d_hi = jnp.exp(jnp.where(row >= s - half, b_hi - bs, NEG))
                acc_hi = acc_hi + jnp.sum(q_hi * ks * d_hi, axis=-1, keepdims=True) * vs
            st = st_ref[h]
            inter = lax.dot_general(qd_s[h, pl.ds(r0, T), :], st, nt, preferred_element_type=F32)
            o_s[h, pl.ds(r0, half), :] = acc_lo + inter[:half]
            o_s[h, pl.ds(r1, half), :] = acc_hi + inter[half:]
            upd = lax.dot_general(v_s[h, pl.ds(r0, T), :], kd_s[h, pl.ds(r0, T), :], tn,
                                  preferred_element_type=F32)
            st_ref[h] = st * es_s[h, pl.ds(r0, 1), :] + upd
        return carry

    lax.fori_loop(0, lc // T, sub, 0)

    gain = gain_ref[...]
    for h in range(H):
        cs = slice(h * HG_DV, (h + 1) * HG_DV)
        hg = hg_ref[0, :, cs]
        o_ref[0, :, cs] = _rms(o_s[h], gain) * (hg * jax.nn.sigmoid(hg))


def _hgrn(zh, lb_table, gain, layer, lc):
    B, S, _ = zh.shape
    W = HG_HEADS * HG_DK
    blk = lambda part: pl.BlockSpec((1, lc, W), lambda b, c: (b, c, part))
    scr = lambda: pltpu.VMEM((HG_HEADS, lc, HG_DK), F32)
    return pl.pallas_call(
        functools.partial(_hgrn_body, lc=lc, layer=layer),
        grid=(B, S // lc),
        in_specs=[
            blk(0), blk(1), blk(2), blk(3),
            pl.BlockSpec(lb_table.shape, lambda b, c: (0, 0)),
            pl.BlockSpec((1, HG_DV), lambda b, c: (0, 0)),
        ],
        out_specs=pl.BlockSpec((1, lc, W), lambda b, c: (b, c, 0)),
        out_shape=jax.ShapeDtypeStruct((B, S, W), F32),
        scratch_shapes=[pltpu.VMEM((HG_HEADS, HG_DV, HG_DK), F32)] + [scr() for _ in range(8)],
        compiler_params=_params("parallel", "arbitrary"),
        name="hgrn2",
    )(zh, zh, zh, zh, lb_table, gain)


def _mix_out_body(a_ref, r_ref, x_ref, gm_ref, wa_ref, wr_ref, gp_ref, o_ref):
    an = _rms(a_ref[...], gm_ref[...]).astype(BF16)
    y = jnp.dot(an, wa_ref[...], preferred_element_type=F32)
    y = y + jnp.dot(r_ref[...].astype(BF16), wr_ref[...], preferred_element_type=F32)
    o_ref[...] = x_ref[...] + _rms(y, gp_ref[...])


def _mix_out(a2d, r2d, x2d, gm, wa, wr, gp, tm):
    N, D = x2d.shape
    return pl.pallas_call(
        _mix_out_body,
        grid=(N // tm,),
        in_specs=[
            pl.BlockSpec((tm, a2d.shape[1]), lambda i: (i, 0)),
            pl.BlockSpec((tm, r2d.shape[1]), lambda i: (i, 0)),
            pl.BlockSpec((tm, D), lambda i: (i, 0)),
            _const_spec(gm.shape), _const_spec(wa.shape), _const_spec(wr.shape), _const_spec(gp.shape),
        ],
        out_specs=pl.BlockSpec((tm, D), lambda i: (i, 0)),
        out_shape=jax.ShapeDtypeStruct((N, D), F32),
        compiler_params=_params("parallel"),
        name="mix_out",
    )(a2d, r2d, x2d, gm, wa, wr, gp)


def _xattn_body(h_ref, gpre_ref, wq_ref, kv_ref, wo_ref, gpost_ref, o_ref, oh_ref, *, q_scale):
    h = h_ref[...]
    D = h.shape[-1]
    dh = D // X_HEADS
    xn = _rms(h, gpre_ref[...]).astype(BF16)
    qx = (jnp.dot(xn, wq_ref[...], preferred_element_type=F32) * q_scale).astype(BF16)
    for hd in range(X_HEADS):
        qh = qx[:, hd * dh:(hd + 1) * dh]
        kh = kv_ref[0, :, hd * dh:(hd + 1) * dh]
        vh = kv_ref[0, :, D + hd * dh:D + (hd + 1) * dh]
        s = lax.dot_general(qh, kh, (((1,), (1,)), ((), ())), preferred_element_type=F32)
        p = jnp.exp2(s - jnp.max(s, axis=-1, keepdims=True))
        l = jnp.sum(p, axis=-1, keepdims=True)
        oh = jnp.dot(p.astype(BF16), vh, preferred_element_type=F32) / l
        oh_ref[:, hd * dh:(hd + 1) * dh] = oh.astype(BF16)
    ox = jnp.dot(oh_ref[...], wo_ref[...], preferred_element_type=F32)
    o_ref[...] = h + _rms(ox, gpost_ref[...])


def _xattn(h2d, gpre, wq, kvm, wo, gpost, tm, tiles_per_batch):
    N, D = h2d.shape
    M = kvm.shape[1]
    q_scale = ((D // X_HEADS) ** -0.5) * LOG2E
    return pl.pallas_call(
        functools.partial(_xattn_body, q_scale=q_scale),
        grid=(N // tm,),
        in_specs=[
            pl.BlockSpec((tm, D), lambda i: (i, 0)),
            _const_spec(gpre.shape), _const_spec(wq.shape),
            pl.BlockSpec((1, M, 2 * D), lambda i: (i // tiles_per_batch, 0, 0)),
            _const_spec(wo.shape), _const_spec(gpost.shape),
        ],
        out_specs=pl.BlockSpec((tm, D), lambda i: (i, 0)),
        out_shape=jax.ShapeDtypeStruct((N, D), F32),
        scratch_shapes=[pltpu.VMEM((tm, D), BF16)],
        compiler_params=_params("parallel"),
        name="xattn",
    )(h2d, gpre, wq, kvm, wo, gpost)


def _gelu_tanh(x):
    return x * (0.5 * (1.0 + jnp.tanh(math.sqrt(2.0 / math.pi) * (x + 0.044715 * (x * x * x)))))


def _ffn_up_body(h_ref, gpre_ref, wg_ref, wv_ref, cwg_ref, cwv_ref, cbg_ref, cbv_ref,
                 act_ref, xn_ref, cg_ref, cv_ref, bg0_ref, bv0_ref, bg1_ref, bv1_ref,
                 *, tm, nj, tiles_per_batch):
    i = pl.program_id(0)
    j = pl.program_id(1)
    first = (i % tiles_per_batch) == 0
    slots = ((bg0_ref, bv0_ref), (bg1_ref, bv1_ref))

    def project(slot):
        for w_ref, carry_ref, buf_ref in zip((wg_ref, wv_ref), (cg_ref, cv_ref), slots[slot]):
            u = jnp.dot(xn_ref[...], w_ref[0], preferred_element_type=F32)
            buf_ref[0:CONV_PAD, :] = jnp.where(first, 0.0, carry_ref[j])
            buf_ref[CONV_PAD:CONV_PAD + tm, :] = u
            carry_ref[j] = u[tm - CONV_PAD:, :]

    def conv(buf_ref, cw, cb, r):
        base = CONV_PAD + r
        return (cw[0:1, :] * buf_ref[base - 2:base - 2 + FF_CHUNK, :]
                + cw[1:2, :] * buf_ref[base - 1:base - 1 + FF_CHUNK, :]
                + cw[2:3, :] * buf_ref[base:base + FF_CHUNK, :] + cb)

    def finish(slot):
        bg_ref, bv_ref = slots[slot]
        cwg = cwg_ref[...]
        cwv = cwv_ref[...]
        cbg = cbg_ref[...]
        cbv = cbv_ref[...]
        for r in range(0, tm, FF_CHUNK):
            gate = conv(bg_ref, cwg, cbg, r)
            val = conv(bv_ref, cwv, cbv, r)
            act_ref[r:r + FF_CHUNK, :] = (_gelu_tanh(gate) * val).astype(BF16)

    @pl.when(j == 0)
    def _():
        @pl.when(i == 0)
        def _():
            cg_ref[...] = jnp.zeros(cg_ref.shape, F32)
            cv_ref[...] = jnp.zeros(cv_ref.shape, F32)

        xn_ref[...] = _rms(h_ref[...], gpre_ref[...]).astype(BF16)
        project(0)

    for parity in (0, 1):
        @pl.when((j > 0) & (j < nj) & (j % 2 == parity))
        def _():
            finish(1 - parity)
            project(parity)

    @pl.when(j == nj)
    def _():
        finish((nj - 1) % 2)


def _ffn_up(h2d, gpre, wg, wv, cwg, cwv, cbg, cbv, tm, tf, tiles_per_batch):
    N, D = h2d.shape
    nj = wg.shape[0]
    F = nj * tf
    cur = lambda i, j: (jnp.minimum(j, nj - 1), 0, 0)
    prev = lambda i, j: (0, jnp.maximum(j - 1, 0))
    buf = lambda: pltpu.VMEM((CONV_PAD + tm, tf), F32)
    return pl.pallas_call(
        functools.partial(_ffn_up_body, tm=tm, nj=nj, tiles_per_batch=tiles_per_batch),
        grid=(N // tm, nj + 1),
        in_specs=[
            pl.BlockSpec((tm, D), lambda i, j: (i, 0)),
            _const_spec(gpre.shape),
            pl.BlockSpec((1, D, tf), cur),
            pl.BlockSpec((1, D, tf), cur),
            pl.BlockSpec((CONV_WIDTH, tf), prev), pl.BlockSpec((CONV_WIDTH, tf), prev),
            pl.BlockSpec((1, tf), prev), pl.BlockSpec((1, tf), prev),
        ],
        out_specs=pl.BlockSpec((tm, tf), lambda i, j: (i, jnp.maximum(j - 1, 0))),
        out_shape=jax.ShapeDtypeStruct((N, F), BF16),
        scratch_shapes=[
            pltpu.VMEM((tm, D), BF16),
            pltpu.VMEM((nj, CONV_PAD, tf), F32),
            pltpu.VMEM((nj, CONV_PAD, tf), F32),
            buf(), buf(), buf(), buf(),
        ],
        compiler_params=_params("arbitrary", "arbitrary"),
        name="ffn_up",
    )(h2d, gpre, wg, wv, cwg, cwv, cbg, cbv)


def _ffn_down_body(act_ref, h_ref, wd_ref, gpost_ref, o_ref):
    y = jnp.dot(act_ref[...], wd_ref[...], preferred_element_type=F32)
    o_ref[...] = h_ref[...] + _rms(y, gpost_ref[...])


def _ffn_down(act, h2d, wd, gpost, tm):
    N, D = h2d.shape
    F = act.shape[1]
    return pl.pallas_call(
        _ffn_down_body,
        grid=(N // tm,),
        in_specs=[
            pl.BlockSpec((tm, F), lambda i: (i, 0)),
            pl.BlockSpec((tm, D), lambda i: (i, 0)),
            _const_spec(wd.shape), _const_spec(gpost.shape),
        ],
        out_specs=pl.BlockSpec((tm, D), lambda i: (i, 0)),
        out_shape=jax.ShapeDtypeStruct((N, D), F32),
        compiler_params=_params("parallel"),
        name="ffn_down",
    )(act, h2d, wd, gpost)


def kernel(x, mem, w_in, q_norm, w_uq, kv_norm, w_ukv, mla_out_norm, hgrn_lb, hgrn_out_norm, w_out,
           ln_mix_pre, ln_mix_post, ln_x_pre, ln_x_post, mem_norm, w_xq, w_xk, w_xv, w_xo,
           ln_ffn_pre, ln_ffn_post, w_up, conv_w, conv_b, w_down):
    B, S, D = x.shape
    M = mem.shape[1]
    depth = w_in.shape[0]
    N = B * S
    tm = min(ROW_TILE, S)
    blk = min(ATT_BLK, S)
    bq = min(ATT_QBLK, S)
    lc = min(HG_BLK, S)
    assert S % tm == 0 and S % blk == 0 and S % bq == 0 and bq % blk == 0 and blk % CHUNK == 0
    assert S % lc == 0 and lc % HG_SUB == 0
    tiles_per_batch = S // tm
    d_ff = w_down.shape[1]
    tf = FF_TILE
    tmf = min(FF_ROWS, S)
    assert d_ff % tf == 0 and S % tmf == 0 and tmf % FF_CHUNK == 0
    n_mla_in = MLA_Q_RANK + MLA_KV_RANK + MLA_ROPE

    pos = jnp.arange(S, dtype=F32)
    inv_freq = 1.0 / (ROPE_THETA ** (jnp.arange(0, MLA_ROPE, 2, dtype=F32) / MLA_ROPE))
    ang = pos[:, None] * inv_freq[None, :]
    cos, sin = jnp.cos(ang), jnp.sin(ang)
    cosT, sinT = cos.T, sin.T

    row = lambda v: v.reshape(1, -1).astype(F32)
    h = x
    for l in range(depth):
        w_mla = w_in[l][:, :n_mla_in].astype(BF16)
        w_hg = w_in[l][:, n_mla_in:].astype(BF16)
        w_uqT = w_uq[l].T.astype(BF16)
        w_ukv_h = w_ukv[l].reshape(MLA_KV_RANK, MLA_HEADS, MLA_NOPE + MLA_V)
        w_uk = w_ukv_h[:, :, :MLA_NOPE].reshape(MLA_KV_RANK, MLA_HEADS * MLA_NOPE).astype(BF16)
        w_uvT = w_ukv_h[:, :, MLA_NOPE:].reshape(MLA_KV_RANK, MLA_HEADS * MLA_V).T.astype(BF16)
        w_out_a = w_out[l][:MLA_WIDTH].astype(BF16)
        w_out_r = w_out[l][MLA_WIDTH:].astype(BF16)
        w_xkv = jnp.concatenate([w_xk[l], w_xv[l]], axis=1).astype(BF16)
        tiled = lambda w: w.astype(BF16).reshape(D, d_ff // tf, tf).transpose(1, 0, 2)
        w_up_g = tiled(w_up[l][:, :d_ff])
        w_up_v = tiled(w_up[l][:, d_ff:])

        qT, k, vT = _mla_prep(h, row(ln_mix_pre[l]), w_mla, row(q_norm[l]), w_uqT, row(kv_norm[l]),
                              w_uk, w_uvT, cos, sin, cosT, sinT, blk)
        a = _attention(qT, k, vT, bq, blk)
        zh = _norm_matmul(h.reshape(N, D), row(ln_mix_pre[l]), w_hg, F32, tm, w_hg.shape[1], "hgrn_in")
        r = _hgrn(zh.reshape(B, S, -1), hgrn_lb.astype(F32), row(hgrn_out_norm[l]), l, lc)
        h2d = _mix_out(a.reshape(N, -1), r.reshape(N, -1), h.reshape(N, D), row(mla_out_norm[l]),
                       w_out_a, w_out_r, row(ln_mix_post[l]), tm)

        kvm = _norm_matmul(mem.reshape(B * M, D), row(mem_norm[l]), w_xkv, BF16, min(256, B * M), 1024, "mem_kv")
        h2d = _xattn(h2d, row(ln_x_pre[l]), w_xq[l].astype(BF16), kvm.reshape(B, M, 2 * D),
                     w_xo[l].astype(BF16), row(ln_x_post[l]), tm, tiles_per_batch)

        act = _ffn_up(h2d, row(ln_ffn_pre[l]), w_up_g, w_up_v, conv_w[l][:, :d_ff], conv_w[l][:, d_ff:],
                      row(conv_b[l][:d_ff]), row(conv_b[l][d_ff:]), tmf, tf, S // tmf)
        h2d = _ffn_down(act, h2d, w_down[l].astype(BF16), row(ln_ffn_post[l]), min(FF_DOWN_ROWS, S))
        h = h2d.reshape(B, S, D)
    return h
```

```python
import functools
import math

import jax
import jax.numpy as jnp
from jax import lax
from jax.experimental import pallas as pl
from jax.experimental.pallas import tpu as pltpu

F32 = jnp.float32
BF16 = jnp.bfloat16

EPS = 1e-6
CHUNK = 64
MLA_V = 128
MLA_NOPE = 128
MLA_ROPE = 64
MLA_HEADS = 8
MLA_QK = MLA_NOPE + MLA_ROPE
MLA_Q_RANK = 512
MLA_KV_RANK = 256
ROPE_THETA = 10000.0
HG_DK = 128
HG_DV = 128
HG_HEADS = 8
HG_WIDTH = HG_HEADS * HG_DV
MLA_WIDTH = MLA_HEADS * MLA_V
X_HEADS = 4
CONV_WIDTH = 3

LOG2E = 1.4426950408889634
NEG = -1e30

VMEM_LIMIT_BYTES = 56 * 1024 * 1024

ROW_TILE = 512
ATT_BLK = 1024
ATT_QBLK = 2048
ATT_MIN_MASS = 2.0 ** -80
HG_BLK = 256
HG_SUB = 16
SUBLANES = 8
FF_TILE = 512
FF_ROWS = 1024
FF_CHUNK = 64
FF_DOWN_ROWS = 256
CONV_PAD = 8


def _params(*sem):
    return pltpu.CompilerParams(dimension_semantics=sem, vmem_limit_bytes=VMEM_LIMIT_BYTES)


def _rms(xf, g):
    return xf * lax.rsqrt(jnp.mean(xf * xf, axis=-1, keepdims=True) + EPS) * g


def _const_spec(shape):
    nd = len(shape)
    return pl.BlockSpec(shape, lambda *_: (0,) * nd, pipeline_mode=pl.Buffered(1))


def _mla_prep_body(x_ref, g_ref, wm_ref, qn_ref, wuqT_ref, kvn_ref, wuk_ref, wuvT_ref,
                   cos_ref, sin_ref, cosT_ref, sinT_ref, qT_ref, k_ref, vT_ref, *, q_scale):
    xn = _rms(x_ref[0], g_ref[...]).astype(BF16)
    z = jnp.dot(xn, wm_ref[...], preferred_element_type=F32)
    cq = _rms(z[:, :MLA_Q_RANK], qn_ref[...]).astype(BF16)
    ckv = _rms(z[:, MLA_Q_RANK:MLA_Q_RANK + MLA_KV_RANK], kvn_ref[...]).astype(BF16)
    kr = z[:, MLA_Q_RANK + MLA_KV_RANK:]

    nt = (((1,), (1,)), ((), ()))
    qT = lax.dot_general(wuqT_ref[...], cq, nt, preferred_element_type=F32)
    cosT = cosT_ref[...]
    sinT = sinT_ref[...]
    half = MLA_ROPE // 2
    for h in range(MLA_HEADS):
        base = h * MLA_QK
        x1 = qT[base + MLA_NOPE:base + MLA_NOPE + half]
        x2 = qT[base + MLA_NOPE + half:base + MLA_QK]
        qT_ref[0, base:base + MLA_NOPE, :] = (qT[base:base + MLA_NOPE] * q_scale).astype(BF16)
        qT_ref[0, base + MLA_NOPE:base + MLA_NOPE + half, :] = ((x1 * cosT - x2 * sinT) * q_scale).astype(BF16)
        qT_ref[0, base + MLA_NOPE + half:base + MLA_QK, :] = ((x2 * cosT + x1 * sinT) * q_scale).astype(BF16)

    kn = jnp.dot(ckv, wuk_ref[...], preferred_element_type=F32)
    cos = cos_ref[...]
    sin = sin_ref[...]
    k1 = kr[:, :half]
    k2 = kr[:, half:]
    krope = jnp.concatenate([k1 * cos - k2 * sin, k2 * cos + k1 * sin], axis=-1).astype(BF16)
    for h in range(MLA_HEADS):
        k_ref[0, h, :, 0:MLA_NOPE] = kn[:, h * MLA_NOPE:(h + 1) * MLA_NOPE].astype(BF16)
        k_ref[0, h, :, MLA_NOPE:MLA_QK] = krope

    vT = lax.dot_general(wuvT_ref[...], ckv, nt, preferred_element_type=F32)
    for h in range(MLA_HEADS):
        vT_ref[0, h, 0] = vT[h * MLA_V:(h + 1) * MLA_V].astype(BF16)


def _mla_prep(x, g, wm, qn, wuqT, kvn, wuk, wuvT, cos, sin, cosT, sinT, blk):
    B, S, D = x.shape
    nb = S // blk
    half = MLA_ROPE // 2
    q_scale = (MLA_QK ** -0.5) * LOG2E
    return pl.pallas_call(
        functools.partial(_mla_prep_body, q_scale=q_scale),
        grid=(B, nb),
        in_specs=[
            pl.BlockSpec((1, blk, D), lambda b, i: (b, i, 0)),
            _const_spec(g.shape), _const_spec(wm.shape), _const_spec(qn.shape), _const_spec(wuqT.shape),
            _const_spec(kvn.shape), _const_spec(wuk.shape), _const_spec(wuvT.shape),
            pl.BlockSpec((blk, half), lambda b, i: (i, 0)),
            pl.BlockSpec((blk, half), lambda b, i: (i, 0)),
            pl.BlockSpec((half, blk), lambda b, i: (0, i)),
            pl.BlockSpec((half, blk), lambda b, i: (0, i)),
        ],
        out_specs=[
            pl.BlockSpec((1, MLA_HEADS * MLA_QK, blk), lambda b, i: (b, 0, i)),
            pl.BlockSpec((1, MLA_HEADS, blk, MLA_QK), lambda b, i: (b, 0, i, 0)),
            pl.BlockSpec((1, MLA_HEADS, 1, MLA_V, blk), lambda b, i: (b, 0, i, 0, 0)),
        ],
        out_shape=[
            jax.ShapeDtypeStruct((B, MLA_HEADS * MLA_QK, S), BF16),
            jax.ShapeDtypeStruct((B, MLA_HEADS, S, MLA_QK), BF16),
            jax.ShapeDtypeStruct((B, MLA_HEADS, nb, MLA_V, blk), BF16),
        ],
        compiler_params=_params("parallel", "parallel"),
        name="mla_prep",
    )(x, g, wm, qn, wuqT, kvn, wuk, wuvT, cos, sin, cosT, sinT)


def _attn_body(qT_ref, k_ref, vT_ref, o_ref, kn_ref, m_ref, l_ref, acc_ref, *, bq, bk):
    i = pl.program_id(2)
    ratio = bq // bk
    qT = qT_ref[0]

    @pl.when(i == 0)
    def _():
        def kmax(c, best):
            kc = k_ref[0, 0, pl.ds(pl.multiple_of(c * bk, bk), bk), :].astype(F32)
            sq = jnp.sum(kc * kc, axis=-1, keepdims=True)
            return jnp.maximum(best, jnp.max(sq, axis=0, keepdims=True))

        best = lax.fori_loop(0, k_ref.shape[2] // bk, kmax, jnp.zeros((1, 1), F32))
        kn_ref[...] = jnp.broadcast_to(jnp.sqrt(best), kn_ref.shape)

    def scores(j, c0, masked):
        kblk = k_ref[0, 0, pl.ds(pl.multiple_of(j * bk, bk), bk), :]
        s = jnp.dot(kblk, qT[:, c0:], preferred_element_type=F32)
        if masked:
            key_chunk = lax.broadcasted_iota(jnp.int32, s.shape, 0) // CHUNK
            qry_chunk = lax.broadcasted_iota(jnp.int32, s.shape, 1) // CHUNK
            s = jnp.where(key_chunk <= qry_chunk, s, NEG)
        return s

    def sweep(update):
        l_ref[...] = jnp.zeros(l_ref.shape, F32)
        acc_ref[...] = jnp.zeros(acc_ref.shape, F32)

        def body(j, carry):
            update(j, 0, False)
            return carry

        lax.fori_loop(0, i * ratio, body, 0)
        for d in range(ratio):
            update(i * ratio + d, d * bk, True)

    def fixed_update(j, c0, masked):
        p = jnp.exp2(scores(j, c0, masked) - m_ref[:, c0:])
        l_ref[:, c0:] += jnp.sum(p, axis=0, keepdims=True)
        acc_ref[:, c0:] += jnp.dot(vT_ref[0, 0, j], p.astype(BF16), preferred_element_type=F32)

    qf = qT.astype(F32)
    m_ref[...] = jnp.sqrt(jnp.sum(qf * qf, axis=0, keepdims=True)) * kn_ref[...] + 1.0
    sweep(fixed_update)

    def online_update(j, c0, masked):
        s = scores(j, c0, masked)
        m_prev = m_ref[:, c0:]
        m_new = jnp.maximum(m_prev, jnp.max(s, axis=0, keepdims=True))
        alpha = jnp.exp2(m_prev - m_new)
        p = jnp.exp2(s - m_new)
        l_ref[:, c0:] = alpha * l_ref[:, c0:] + jnp.sum(p, axis=0, keepdims=True)
        acc_ref[:, c0:] = alpha * acc_ref[:, c0:] + jnp.dot(vT_ref[0, 0, j], p.astype(BF16),
                                                            preferred_element_type=F32)
        m_ref[:, c0:] = m_new

    @pl.when(jnp.logical_not(jnp.min(l_ref[...]) >= ATT_MIN_MASS))
    def _():
        m_ref[...] = jnp.full(m_ref.shape, NEG, F32)
        sweep(online_update)

    o_ref[0] = (acc_ref[...] / l_ref[...]).T


def _attention(qT, k, vT, bq, bk):
    B, H, S, _ = k.shape
    return pl.pallas_call(
        functools.partial(_attn_body, bq=bq, bk=bk),
        grid=(B, H, S // bq),
        in_specs=[
            pl.BlockSpec((1, MLA_QK, bq), lambda b, h, i: (b, h, i)),
            pl.BlockSpec((1, 1, S, MLA_QK), lambda b, h, i: (b, h, 0, 0)),
            pl.BlockSpec((1, 1, S // bk, MLA_V, bk), lambda b, h, i: (b, h, 0, 0, 0)),
        ],
        out_specs=pl.BlockSpec((1, bq, MLA_V), lambda b, h, i: (b, i, h)),
        out_shape=jax.ShapeDtypeStruct((B, S, H * MLA_V), F32),
        scratch_shapes=[pltpu.VMEM((1, bq), F32), pltpu.VMEM((1, bq), F32), pltpu.VMEM((1, bq), F32),
                        pltpu.VMEM((MLA_V, bq), F32)],
        compiler_params=_params("parallel", "parallel", "arbitrary"),
        name="mla_attention",
    )(qT, k, vT)


def _norm_matmul_body(x_ref, g_ref, w_ref, o_ref, xn_ref):
    @pl.when(pl.program_id(1) == 0)
    def _():
        xn_ref[...] = _rms(x_ref[...], g_ref[...]).astype(BF16)

    o_ref[...] = jnp.dot(xn_ref[...], w_ref[...], preferred_element_type=F32).astype(o_ref.dtype)


def _norm_matmul(x2d, g, w, out_dtype, tm, tn, name):
    N, D = x2d.shape
    M = w.shape[1]
    w_spec = _const_spec(w.shape) if tn == M else pl.BlockSpec((D, tn), lambda i, j: (0, j))
    return pl.pallas_call(
        _norm_matmul_body,
        grid=(N // tm, M // tn),
        in_specs=[
            pl.BlockSpec((tm, D), lambda i, j: (i, 0)),
            _const_spec(g.shape),
            w_spec,
        ],
        out_specs=pl.BlockSpec((tm, tn), lambda i, j: (i, j)),
        out_shape=jax.ShapeDtypeStruct((N, M), out_dtype),
        scratch_shapes=[pltpu.VMEM((tm, D), BF16)],
        compiler_params=_params("parallel", "arbitrary"),
        name=name,
    )(x2d, g, w)


def _group_scan(x, row_in_group, group, reverse):
    n = x.shape[0]
    sh = 1
    while sh < group:
        if reverse:
            x = x + jnp.where(row_in_group < group - sh, pltpu.roll(x, n - sh, axis=0), 0.0)
        else:
            x = x + jnp.where(row_in_group >= sh, pltpu.roll(x, sh, axis=0), 0.0)
        sh *= 2
    return x


def _hgrn_body(hq_ref, hf_ref, hi_ref, hg_ref, lbp_ref, gain_ref, o_ref,
               st_ref, q_s, k_s, b_s, v_s, qd_s, kd_s, es_s, o_s, *, lc, layer):
    T = HG_SUB
    H = HG_HEADS

    @pl.when(pl.program_id(1) == 0)
    def _():
        st_ref[...] = jnp.zeros(st_ref.shape, F32)

    lbp = lbp_ref[...]
    e = jnp.exp(lbp - jnp.max(lbp, axis=0, keepdims=True))
    prob = e / jnp.sum(e, axis=0, keepdims=True)
    lb = jnp.sum(prob[:layer + 1], axis=0, keepdims=True)

    hq = hq_ref[0]
    q = hq * jax.nn.sigmoid(hq)
    f = lb + (1.0 - lb) * jax.nn.sigmoid(hf_ref[0])
    k = 1.0 - f
    logf = jnp.log(f)

    rig = lax.broadcasted_iota(jnp.int32, logf.shape, 0) % T
    b = _group_scan(logf, rig, T, reverse=False)
    suffix = _group_scan(logf, rig, T, reverse=True)
    staged = ((q_s, q), (k_s, k), (b_s, b * LOG2E), (v_s, hi_ref[0]), (qd_s, q * jnp.exp(b)),
              (kd_s, k * jnp.exp(suffix - logf)),
              (es_s, jnp.exp(suffix)))
    for ref, val in staged:
        for h in range(H):
            ref[h] = val[:, h * HG_DK:(h + 1) * HG_DK]

    R = SUBLANES
    groups = T // R
    row = lax.broadcasted_iota(jnp.int32, (R, HG_DK), 0)
    causal_bias = [jnp.where(row >= s, 0.0, NEG) for s in range(R)]
    nt = (((1,), (1,)), ((), ()))
    tn = (((0,), (0,)), ((), ()))

    def sub(i, carry):
        r0 = pl.multiple_of(i * T, T)
        rg = [pl.multiple_of(i * T + g * R, R) for g in range(groups)]
        for h in range(H):
            q_g = [q_s[h, pl.ds(rg[g], R), :] for g in range(groups)]
            b_g = [b_s[h, pl.ds(rg[g], R), :] for g in range(groups)]
            acc = [jnp.zeros((R, HG_DV), F32) for _ in range(groups)]
            for s in range(T):
                ks = k_s[h, pl.ds(r0 + s, 1), :]
                bs = b_s[h, pl.ds(r0 + s, 1), :]
                vs = v_s[h, pl.ds(r0 + s, 1), :]
                for g in range(s // R, groups):
                    diff = b_g[g] - bs
                    if g == s // R:
                        diff = diff + causal_bias[s % R]
                    col = jnp.sum(q_g[g] * ks * jnp.exp2(diff), axis=-1, keepdims=True)
                    acc[g] = acc[g] + col * vs
            st = st_ref[h]
            inter = lax.dot_general(qd_s[h, pl.ds(r0, T), :], st, nt, preferred_element_type=F32)
            for g in range(groups):
                o_s[h, pl.ds(rg[g], R), :] = acc[g] + inter[g * R:(g + 1) * R]
            upd = lax.dot_general(v_s[h, pl.ds(r0, T), :], kd_s[h, pl.ds(r0, T), :], tn,
                                  preferred_element_type=F32)
            st_ref[h] = st * es_s[h, pl.ds(r0, 1), :] + upd
        return carry

    lax.fori_loop(0, lc // T, sub, 0)

    gain = gain_ref[...]
    for h in range(H):
        cs = slice(h * HG_DV, (h + 1) * HG_DV)
        hg = hg_ref[0, :, cs]
        o_ref[0, :, cs] = _rms(o_s[h], gain) * (hg * jax.nn.sigmoid(hg))


def _hgrn(zh, lb_table, gain, layer, lc):
    B, S, _ = zh.shape
    W = HG_HEADS * HG_DK
    blk = lambda part: pl.BlockSpec((1, lc, W), lambda b, c: (b, c, part))
    scr = lambda: pltpu.VMEM((HG_HEADS, lc, HG_DK), F32)
    return pl.pallas_call(
        functools.partial(_hgrn_body, lc=lc, layer=layer),
        grid=(B, S // lc),
        in_specs=[
            blk(0), blk(1), blk(2), blk(3),
            pl.BlockSpec(lb_table.shape, lambda b, c: (0, 0)),
            pl.BlockSpec((1, HG_DV), lambda b, c: (0, 0)),
        ],
        out_specs=pl.BlockSpec((1, lc, W), lambda b, c: (b, c, 0)),
        out_shape=jax.ShapeDtypeStruct((B, S, W), F32),
        scratch_shapes=[pltpu.VMEM((HG_HEADS, HG_DV, HG_DK), F32)] + [scr() for _ in range(8)],
        compiler_params=_params("parallel", "arbitrary"),
        name="hgrn2",
    )(zh, zh, zh, zh, lb_table, gain)


def _mix_out_body(a_ref, r_ref, x_ref, gm_ref, wa_ref, wr_ref, gp_ref, o_ref):
    an = _rms(a_ref[...], gm_ref[...]).astype(BF16)
    y = jnp.dot(an, wa_ref[...], preferred_element_type=F32)
    y = y + jnp.dot(r_ref[...].astype(BF16), wr_ref[...], preferred_element_type=F32)
    o_ref[...] = x_ref[...] + _rms(y, gp_ref[...])


def _mix_out(a2d, r2d, x2d, gm, wa, wr, gp, tm):
    N, D = x2d.shape
    return pl.pallas_call(
        _mix_out_body,
        grid=(N // tm,),
        in_specs=[
            pl.BlockSpec((tm, a2d.shape[1]), lambda i: (i, 0)),
            pl.BlockSpec((tm, r2d.shape[1]), lambda i: (i, 0)),
            pl.BlockSpec((tm, D), lambda i: (i, 0)),
            _const_spec(gm.shape), _const_spec(wa.shape), _const_spec(wr.shape), _const_spec(gp.shape),
        ],
        out_specs=pl.BlockSpec((tm, D), lambda i: (i, 0)),
        out_shape=jax.ShapeDtypeStruct((N, D), F32),
        compiler_params=_params("parallel"),
        name="mix_out",
    )(a2d, r2d, x2d, gm, wa, wr, gp)


def _xattn_body(h_ref, gpre_ref, wq_ref, kv_ref, wo_ref, gpost_ref, o_ref, oh_ref, *, q_scale):
    h = h_ref[...]
    D = h.shape[-1]
    dh = D // X_HEADS
    xn = _rms(h, gpre_ref[...]).astype(BF16)
    qx = (jnp.dot(xn, wq_ref[...], preferred_element_type=F32) * q_scale).astype(BF16)
    for hd in range(X_HEADS):
        qh = qx[:, hd * dh:(hd + 1) * dh]
        kh = kv_ref[0, :, hd * dh:(hd + 1) * dh]
        vh = kv_ref[0, :, D + hd * dh:D + (hd + 1) * dh]
        s = lax.dot_general(qh, kh, (((1,), (1,)), ((), ())), preferred_element_type=F32)
        p = jnp.exp2(s - jnp.max(s, axis=-1, keepdims=True))
        l = jnp.sum(p, axis=-1, keepdims=True)
        oh = jnp.dot(p.astype(BF16), vh, preferred_element_type=F32) / l
        oh_ref[:, hd * dh:(hd + 1) * dh] = oh.astype(BF16)
    ox = jnp.dot(oh_ref[...], wo_ref[...], preferred_element_type=F32)
    o_ref[...] = h + _rms(ox, gpost_ref[...])


def _xattn(h2d, gpre, wq, kvm, wo, gpost, tm, tiles_per_batch):
    N, D = h2d.shape
    M = kvm.shape[1]
    q_scale = ((D // X_HEADS) ** -0.5) * LOG2E
    return pl.pallas_call(
        functools.partial(_xattn_body, q_scale=q_scale),
        grid=(N // tm,),
        in_specs=[
            pl.BlockSpec((tm, D), lambda i: (i, 0)),
            _const_spec(gpre.shape), _const_spec(wq.shape),
            pl.BlockSpec((1, M, 2 * D), lambda i: (i // tiles_per_batch, 0, 0)),
            _const_spec(wo.shape), _const_spec(gpost.shape),
        ],
        out_specs=pl.BlockSpec((tm, D), lambda i: (i, 0)),
        out_shape=jax.ShapeDtypeStruct((N, D), F32),
        scratch_shapes=[pltpu.VMEM((tm, D), BF16)],
        compiler_params=_params("parallel"),
        name="xattn",
    )(h2d, gpre, wq, kvm, wo, gpost)


def _gelu_tanh(x):
    return x * (0.5 * (1.0 + jnp.tanh(math.sqrt(2.0 / math.pi) * (x + 0.044715 * (x * x * x)))))


def _ffn_up_body(h_ref, gpre_ref, wg_ref, wv_ref, cwg_ref, cwv_ref, cbg_ref, cbv_ref,
                 act_ref, xn_ref, cg_ref, cv_ref, bg0_ref, bv0_ref, bg1_ref, bv1_ref,
                 *, tm, nj, tiles_per_batch):
    i = pl.program_id(0)
    j = pl.program_id(1)
    first = (i % tiles_per_batch) == 0
    slots = ((bg0_ref, bv0_ref), (bg1_ref, bv1_ref))

    def project(slot):
        for w_ref, carry_ref, buf_ref in zip((wg_ref, wv_ref), (cg_ref, cv_ref), slots[slot]):
            u = jnp.dot(xn_ref[...], w_ref[...], preferred_element_type=F32)
            buf_ref[0:CONV_PAD, :] = jnp.where(first, 0.0, carry_ref[j])
            buf_ref[CONV_PAD:CONV_PAD + tm, :] = u
            carry_ref[j] = u[tm - CONV_PAD:, :]

    def conv(buf_ref, cw, cb, r):
        base = CONV_PAD + r
        return (cw[0:1, :] * buf_ref[base - 2:base - 2 + FF_CHUNK, :]
                + cw[1:2, :] * buf_ref[base - 1:base - 1 + FF_CHUNK, :]
                + cw[2:3, :] * buf_ref[base:base + FF_CHUNK, :] + cb)

    def finish(slot):
        bg_ref, bv_ref = slots[slot]
        cwg = cwg_ref[...]
        cwv = cwv_ref[...]
        cbg = cbg_ref[...]
        cbv = cbv_ref[...]
        for r in range(0, tm, FF_CHUNK):
            gate = conv(bg_ref, cwg, cbg, r)
            val = conv(bv_ref, cwv, cbv, r)
            act_ref[r:r + FF_CHUNK, :] = (_gelu_tanh(gate) * val).astype(BF16)

    @pl.when(j == 0)
    def _():
        @pl.when(i == 0)
        def _():
            cg_ref[...] = jnp.zeros(cg_ref.shape, F32)
            cv_ref[...] = jnp.zeros(cv_ref.shape, F32)

        xn_ref[...] = _rms(h_ref[...], gpre_ref[...]).astype(BF16)
        project(0)

    for parity in (0, 1):
        @pl.when((j > 0) & (j < nj) & (j % 2 == parity))
        def _():
            finish(1 - parity)
            project(parity)

    @pl.when(j == nj)
    def _():
        finish((nj - 1) % 2)


def _ffn_up(h2d, gpre, wg, wv, cwg, cwv, cbg, cbv, tm, tf, tiles_per_batch):
    N, D = h2d.shape
    F = wg.shape[1]
    nj = F // tf
    cur = lambda i, j: (0, jnp.minimum(j, nj - 1))
    prev = lambda i, j: (0, jnp.maximum(j - 1, 0))
    buf = lambda: pltpu.VMEM((CONV_PAD + tm, tf), F32)
    return pl.pallas_call(
        functools.partial(_ffn_up_body, tm=tm, nj=nj, tiles_per_batch=tiles_per_batch),
        grid=(N // tm, nj + 1),
        in_specs=[
            pl.BlockSpec((tm, D), lambda i, j: (i, 0)),
            _const_spec(gpre.shape),
            pl.BlockSpec((D, tf), cur),
            pl.BlockSpec((D, tf), cur),
            pl.BlockSpec((CONV_WIDTH, tf), prev), pl.BlockSpec((CONV_WIDTH, tf), prev),
            pl.BlockSpec((1, tf), prev), pl.BlockSpec((1, tf), prev),
        ],
        out_specs=pl.BlockSpec((tm, tf), lambda i, j: (i, jnp.maximum(j - 1, 0))),
        out_shape=jax.ShapeDtypeStruct((N, F), BF16),
        scratch_shapes=[
            pltpu.VMEM((tm, D), BF16),
            pltpu.VMEM((nj, CONV_PAD, tf), F32),
            pltpu.VMEM((nj, CONV_PAD, tf), F32),
            buf(), buf(), buf(), buf(),
        ],
        compiler_params=_params("arbitrary", "arbitrary"),
        name="ffn_up",
    )(h2d, gpre, wg, wv, cwg, cwv, cbg, cbv)


def _ffn_down_body(act_ref, h_ref, wd_ref, gpost_ref, o_ref):
    y = jnp.dot(act_ref[...], wd_ref[...], preferred_element_type=F32)
    o_ref[...] = h_ref[...] + _rms(y, gpost_ref[...])


def _ffn_down(act, h2d, wd, gpost, tm):
    N, D = h2d.shape
    F = act.shape[1]
    return pl.pallas_call(
        _ffn_down_body,
        grid=(N // tm,),
        in_specs=[
            pl.BlockSpec((tm, F), lambda i: (i, 0)),
            pl.BlockSpec((tm, D), lambda i: (i, 0)),
            _const_spec(wd.shape), _const_spec(gpost.shape),
        ],
        out_specs=pl.BlockSpec((tm, D), lambda i: (i, 0)),
        out_shape=jax.ShapeDtypeStruct((N, D), F32),
        compiler_params=_params("parallel"),
        name="ffn_down",
    )(act, h2d, wd, gpost)


def kernel(x, mem, w_in, q_norm, w_uq, kv_norm, w_ukv, mla_out_norm, hgrn_lb, hgrn_out_norm, w_out,
           ln_mix_pre, ln_mix_post, ln_x_pre, ln_x_post, mem_norm, w_xq, w_xk, w_xv, w_xo,
           ln_ffn_pre, ln_ffn_post, w_up, conv_w, conv_b, w_down):
    B, S, D = x.shape
    M = mem.shape[1]
    depth = w_in.shape[0]
    N = B * S
    tm = min(ROW_TILE, S)
    blk = min(ATT_BLK, S)
    bq = min(ATT_QBLK, S)
    lc = min(HG_BLK, S)
    assert S % tm == 0 and S % blk == 0 and S % bq == 0 and bq % blk == 0 and blk % CHUNK == 0
    assert S % lc == 0 and lc % HG_SUB == 0
    tiles_per_batch = S // tm
    d_ff = w_down.shape[1]
    tf = FF_TILE
    tmf = min(FF_ROWS, S)
    assert d_ff % tf == 0 and S % tmf == 0 and tmf % FF_CHUNK == 0
    n_mla_in = MLA_Q_RANK + MLA_KV_RANK + MLA_ROPE

    pos = jnp.arange(S, dtype=F32)
    inv_freq = 1.0 / (ROPE_THETA ** (jnp.arange(0, MLA_ROPE, 2, dtype=F32) / MLA_ROPE))
    ang = pos[:, None] * inv_freq[None, :]
    cos, sin = jnp.cos(ang), jnp.sin(ang)
    cosT, sinT = cos.T, sin.T

    row = lambda v: v.reshape(1, -1).astype(F32)
    h = x
    for l in range(depth):
        w_mla = w_in[l][:, :n_mla_in].astype(BF16)
        w_hg = w_in[l][:, n_mla_in:].astype(BF16)
        w_uqT = w_uq[l].T.astype(BF16)
        w_ukv_h = w_ukv[l].reshape(MLA_KV_RANK, MLA_HEADS, MLA_NOPE + MLA_V)
        w_uk = w_ukv_h[:, :, :MLA_NOPE].reshape(MLA_KV_RANK, MLA_HEADS * MLA_NOPE).astype(BF16)
        w_uvT = w_ukv_h[:, :, MLA_NOPE:].reshape(MLA_KV_RANK, MLA_HEADS * MLA_V).T.astype(BF16)
        w_out_a = w_out[l][:MLA_WIDTH].astype(BF16)
        w_out_r = w_out[l][MLA_WIDTH:].astype(BF16)
        w_xkv = jnp.concatenate([w_xk[l], w_xv[l]], axis=1).astype(BF16)
        w_up_g = w_up[l][:, :d_ff].astype(BF16)
        w_up_v = w_up[l][:, d_ff:].astype(BF16)

        qT, k, vT = _mla_prep(h, row(ln_mix_pre[l]), w_mla, row(q_norm[l]), w_uqT, row(kv_norm[l]),
                              w_uk, w_uvT, cos, sin, cosT, sinT, blk)
        a = _attention(qT, k, vT, bq, blk)
        zh = _norm_matmul(h.reshape(N, D), row(ln_mix_pre[l]), w_hg, F32, tm, w_hg.shape[1], "hgrn_in")
        r = _hgrn(zh.reshape(B, S, -1), hgrn_lb.astype(F32), row(hgrn_out_norm[l]), l, lc)
        h2d = _mix_out(a.reshape(N, -1), r.reshape(N, -1), h.reshape(N, D), row(mla_out_norm[l]),
                       w_out_a, w_out_r, row(ln_mix_post[l]), tm)

        kvm = _norm_matmul(mem.reshape(B * M, D), row(mem_norm[l]), w_xkv, BF16, min(256, B * M), 1024, "mem_kv")
        h2d = _xattn(h2d, row(ln_x_pre[l]), w_xq[l].astype(BF16), kvm.reshape(B, M, 2 * D),
                     w_xo[l].astype(BF16), row(ln_x_post[l]), tm, tiles_per_batch)

        act = _ffn_up(h2d, row(ln_ffn_pre[l]), w_up_g, w_up_v, conv_w[l][:, :d_ff], conv_w[l][:, d_ff:],
                      row(conv_b[l][:d_ff]), row(conv_b[l][d_ff:]), tmf, tf, S // tmf)
        h2d = _ffn_down(act, h2d, w_down[l].astype(BF16), row(ln_ffn_post[l]), min(FF_DOWN_ROWS, S))
        h = h2d.reshape(B, S, D)
    return h
```

```python
import functools
import math

import jax
import jax.numpy as jnp
from jax import lax
from jax.experimental import pallas as pl
from jax.experimental.pallas import tpu as pltpu

F32 = jnp.float32
BF16 = jnp.bfloat16

EPS = 1e-6
CHUNK = 64
MLA_V = 128
MLA_NOPE = 128
MLA_ROPE = 64
MLA_HEADS = 8
MLA_QK = MLA_NOPE + MLA_ROPE
MLA_Q_RANK = 512
MLA_KV_RANK = 256
ROPE_THETA = 10000.0
HG_DK = 128
HG_DV = 128
HG_HEADS = 8
HG_WIDTH = HG_HEADS * HG_DV
MLA_WIDTH = MLA_HEADS * MLA_V
X_HEADS = 4
CONV_WIDTH = 3

LOG2E = 1.4426950408889634
NEG = -1e30

VMEM_LIMIT_BYTES = 56 * 1024 * 1024

ROW_TILE = 512
ATT_BLK = 1024
ATT_QBLK = 2048
ATT_MIN_MASS = 2.0 ** -80
HG_BLK = 256
HG_SUB = 16
SUBLANES = 8
FF_TILE = 512
FF_ROWS = 1024
FF_CHUNK = 64
FF_DOWN_ROWS = 256
CONV_PAD = 8


def _params(*sem):
    return pltpu.CompilerParams(dimension_semantics=sem, vmem_limit_bytes=VMEM_LIMIT_BYTES)


def _rms(xf, g):
    return xf * lax.rsqrt(jnp.mean(xf * xf, axis=-1, keepdims=True) + EPS) * g


def _const_spec(shape):
    nd = len(shape)
    return pl.BlockSpec(shape, lambda *_: (0,) * nd, pipeline_mode=pl.Buffered(1))


def _mla_prep_body(x_ref, g_ref, wm_ref, qn_ref, wuqT_ref, kvn_ref, wuk_ref, wuvT_ref,
                   cos_ref, sin_ref, cosT_ref, sinT_ref, qT_ref, k_ref, vT_ref, *, q_scale):
    xn = _rms(x_ref[0], g_ref[...]).astype(BF16)
    z = jnp.dot(xn, wm_ref[...], preferred_element_type=F32)
    cq = _rms(z[:, :MLA_Q_RANK], qn_ref[...]).astype(BF16)
    ckv = _rms(z[:, MLA_Q_RANK:MLA_Q_RANK + MLA_KV_RANK], kvn_ref[...]).astype(BF16)
    kr = z[:, MLA_Q_RANK + MLA_KV_RANK:]

    nt = (((1,), (1,)), ((), ()))
    qT = lax.dot_general(wuqT_ref[...], cq, nt, preferred_element_type=F32)
    cosT = cosT_ref[...]
    sinT = sinT_ref[...]
    half = MLA_ROPE // 2
    for h in range(MLA_HEADS):
        base = h * MLA_QK
        x1 = qT[base + MLA_NOPE:base + MLA_NOPE + half]
        x2 = qT[base + MLA_NOPE + half:base + MLA_QK]
        qT_ref[0, base:base + MLA_NOPE, :] = (qT[base:base + MLA_NOPE] * q_scale).astype(BF16)
        qT_ref[0, base + MLA_NOPE:base + MLA_NOPE + half, :] = ((x1 * cosT - x2 * sinT) * q_scale).astype(BF16)
        qT_ref[0, base + MLA_NOPE + half:base + MLA_QK, :] = ((x2 * cosT + x1 * sinT) * q_scale).astype(BF16)

    kn = jnp.dot(ckv, wuk_ref[...], preferred_element_type=F32)
    cos = cos_ref[...]
    sin = sin_ref[...]
    k1 = kr[:, :half]
    k2 = kr[:, half:]
    krope = jnp.concatenate([k1 * cos - k2 * sin, k2 * cos + k1 * sin], axis=-1).astype(BF16)
    for h in range(MLA_HEADS):
        k_ref[0, h, :, 0:MLA_NOPE] = kn[:, h * MLA_NOPE:(h + 1) * MLA_NOPE].astype(BF16)
        k_ref[0, h, :, MLA_NOPE:MLA_QK] = krope

    vT = lax.dot_general(wuvT_ref[...], ckv, nt, preferred_element_type=F32)
    for h in range(MLA_HEADS):
        vT_ref[0, h, 0] = vT[h * MLA_V:(h + 1) * MLA_V].astype(BF16)


def _mla_prep(x, g, wm, qn, wuqT, kvn, wuk, wuvT, cos, sin, cosT, sinT, blk):
    B, S, D = x.shape
    nb = S // blk
    half = MLA_ROPE // 2
    q_scale = (MLA_QK ** -0.5) * LOG2E
    return pl.pallas_call(
        functools.partial(_mla_prep_body, q_scale=q_scale),
        grid=(B, nb),
        in_specs=[
            pl.BlockSpec((1, blk, D), lambda b, i: (b, i, 0)),
            _const_spec(g.shape), _const_spec(wm.shape), _const_spec(qn.shape), _const_spec(wuqT.shape),
            _const_spec(kvn.shape), _const_spec(wuk.shape), _const_spec(wuvT.shape),
            pl.BlockSpec((blk, half), lambda b, i: (i, 0)),
            pl.BlockSpec((blk, half), lambda b, i: (i, 0)),
            pl.BlockSpec((half, blk), lambda b, i: (0, i)),
            pl.BlockSpec((half, blk), lambda b, i: (0, i)),
        ],
        out_specs=[
            pl.BlockSpec((1, MLA_HEADS * MLA_QK, blk), lambda b, i: (b, 0, i)),
            pl.BlockSpec((1, MLA_HEADS, blk, MLA_QK), lambda b, i: (b, 0, i, 0)),
            pl.BlockSpec((1, MLA_HEADS, 1, MLA_V, blk), lambda b, i: (b, 0, i, 0, 0)),
        ],
        out_shape=[
            jax.ShapeDtypeStruct((B, MLA_HEADS * MLA_QK, S), BF16),
            jax.ShapeDtypeStruct((B, MLA_HEADS, S, MLA_QK), BF16),
            jax.ShapeDtypeStruct((B, MLA_HEADS, nb, MLA_V, blk), BF16),
        ],
        compiler_params=_params("parallel", "parallel"),
        name="mla_prep",
    )(x, g, wm, qn, wuqT, kvn, wuk, wuvT, cos, sin, cosT, sinT)


def _attn_body(qT_ref, k_ref, vT_ref, o_ref, kn_ref, m_ref, l_ref, acc_ref, *, bq, bk):
    i = pl.program_id(2)
    ratio = bq // bk
    qT = qT_ref[0]

    @pl.when(i == 0)
    def _():
        def kmax(c, best):
            kc = k_ref[0, 0, pl.ds(pl.multiple_of(c * bk, bk), bk), :].astype(F32)
            sq = jnp.sum(kc * kc, axis=-1, keepdims=True)
            return jnp.maximum(best, jnp.max(sq, axis=0, keepdims=True))

        best = lax.fori_loop(0, k_ref.shape[2] // bk, kmax, jnp.zeros((1, 1), F32))
        kn_ref[...] = jnp.broadcast_to(jnp.sqrt(best), kn_ref.shape)

    def scores(j, c0, masked):
        kblk = k_ref[0, 0, pl.ds(pl.multiple_of(j * bk, bk), bk), :]
        s = jnp.dot(kblk, qT[:, c0:], preferred_element_type=F32)
        if masked:
            key_chunk = lax.broadcasted_iota(jnp.int32, s.shape, 0) // CHUNK
            qry_chunk = lax.broadcasted_iota(jnp.int32, s.shape, 1) // CHUNK
            s = jnp.where(key_chunk <= qry_chunk, s, NEG)
        return s

    def sweep(update):
        l_ref[...] = jnp.zeros(l_ref.shape, F32)
        acc_ref[...] = jnp.zeros(acc_ref.shape, F32)

        def body(j, carry):
            update(j, 0, False)
            return carry

        lax.fori_loop(0, i * ratio, body, 0)
        for d in range(ratio):
            update(i * ratio + d, d * bk, True)

    def fixed_update(j, c0, masked):
        p = jnp.exp2(scores(j, c0, masked) - m_ref[:, c0:])
        l_ref[:, c0:] += jnp.sum(p, axis=0, keepdims=True)
        acc_ref[:, c0:] += jnp.dot(vT_ref[0, 0, j], p.astype(BF16), preferred_element_type=F32)

    qf = qT.astype(F32)
    m_ref[...] = jnp.sqrt(jnp.sum(qf * qf, axis=0, keepdims=True)) * kn_ref[...] + 1.0
    sweep(fixed_update)

    def online_update(j, c0, masked):
        s = scores(j, c0, masked)
        m_prev = m_ref[:, c0:]
        m_new = jnp.maximum(m_prev, jnp.max(s, axis=0, keepdims=True))
        alpha = jnp.exp2(m_prev - m_new)
        p = jnp.exp2(s - m_new)
        l_ref[:, c0:] = alpha * l_ref[:, c0:] + jnp.sum(p, axis=0, keepdims=True)
        acc_ref[:, c0:] = alpha * acc_ref[:, c0:] + jnp.dot(vT_ref[0, 0, j], p.astype(BF16),
                                                            preferred_element_type=F32)
        m_ref[:, c0:] = m_new

    @pl.when(jnp.logical_not(jnp.min(l_ref[...]) >= ATT_MIN_MASS))
    def _():
        m_ref[...] = jnp.full(m_ref.shape, NEG, F32)
        sweep(online_update)

    o_ref[0] = (acc_ref[...] / l_ref[...]).T


def _attention(qT, k, vT, bq, bk):
    B, H, S, _ = k.shape
    return pl.pallas_call(
        functools.partial(_attn_body, bq=bq, bk=bk),
        grid=(B, H, S // bq),
        in_specs=[
            pl.BlockSpec((1, MLA_QK, bq), lambda b, h, i: (b, h, i)),
            pl.BlockSpec((1, 1, S, MLA_QK), lambda b, h, i: (b, h, 0, 0)),
            pl.BlockSpec((1, 1, S // bk, MLA_V, bk), lambda b, h, i: (b, h, 0, 0, 0)),
        ],
        out_specs=pl.BlockSpec((1, bq, MLA_V), lambda b, h, i: (b, i, h)),
        out_shape=jax.ShapeDtypeStruct((B, S, H * MLA_V), F32),
        scratch_shapes=[pltpu.VMEM((1, bq), F32), pltpu.VMEM((1, bq), F32), pltpu.VMEM((1, bq), F32),
                        pltpu.VMEM((MLA_V, bq), F32)],
        compiler_params=_params("parallel", "parallel", "arbitrary"),
        name="mla_attention",
    )(qT, k, vT)


def _norm_matmul_body(x_ref, g_ref, w_ref, o_ref, xn_ref):
    @pl.when(pl.program_id(1) == 0)
    def _():
        xn_ref[...] = _rms(x_ref[...], g_ref[...]).astype(BF16)

    o_ref[...] = jnp.dot(xn_ref[...], w_ref[...], preferred_element_type=F32).astype(o_ref.dtype)


def _norm_matmul(x2d, g, w, out_dtype, tm, tn, name):
    N, D = x2d.shape
    M = w.shape[1]
    w_spec = _const_spec(w.shape) if tn == M else pl.BlockSpec((D, tn), lambda i, j: (0, j))
    return pl.pallas_call(
        _norm_matmul_body,
        grid=(N // tm, M // tn),
        in_specs=[
            pl.BlockSpec((tm, D), lambda i, j: (i, 0)),
            _const_spec(g.shape),
            w_spec,
        ],
        out_specs=pl.BlockSpec((tm, tn), lambda i, j: (i, j)),
        out_shape=jax.ShapeDtypeStruct((N, M), out_dtype),
        scratch_shapes=[pltpu.VMEM((tm, D), BF16)],
        compiler_params=_params("parallel", "arbitrary"),
        name=name,
    )(x2d, g, w)


def _group_scan(x, row_in_group, group, reverse):
    n = x.shape[0]
    sh = 1
    while sh < group:
        if reverse:
            x = x + jnp.where(row_in_group < group - sh, pltpu.roll(x, n - sh, axis=0), 0.0)
        else:
            x = x + jnp.where(row_in_group >= sh, pltpu.roll(x, sh, axis=0), 0.0)
        sh *= 2
    return x


def _hgrn_body(hq_ref, hf_ref, hi_ref, hg_ref, lbp_ref, gain_ref, o_ref,
               st_ref, q_s, k_s, b_s, v_s, qd_s, kd_s, es_s, o_s, *, lc, layer):
    T = HG_SUB
    H = HG_HEADS

    @pl.when(pl.program_id(1) == 0)
    def _():
        st_ref[...] = jnp.zeros(st_ref.shape, F32)

    lbp = lbp_ref[...]
    e = jnp.exp(lbp - jnp.max(lbp, axis=0, keepdims=True))
    prob = e / jnp.sum(e, axis=0, keepdims=True)
    lb = jnp.sum(prob[:layer + 1], axis=0, keepdims=True)

    hq = hq_ref[0]
    q = hq * jax.nn.sigmoid(hq)
    f = lb + (1.0 - lb) * jax.nn.sigmoid(hf_ref[0])
    k = 1.0 - f
    logf = jnp.log(f)

    rig = lax.broadcasted_iota(jnp.int32, logf.shape, 0) % T
    b = _group_scan(logf, rig, T, reverse=False)
    suffix = _group_scan(logf, rig, T, reverse=True)
    staged = ((q_s, q), (k_s, k), (b_s, b * LOG2E), (v_s, hi_ref[0]), (qd_s, q * jnp.exp(b)),
              (kd_s, k * jnp.exp(suffix - logf)),
              (es_s, jnp.exp(suffix)))
    for ref, val in staged:
        for h in range(H):
            ref[h] = val[:, h * HG_DK:(h + 1) * HG_DK]

    R = SUBLANES
    groups = T // R
    row = lax.broadcasted_iota(jnp.int32, (R, HG_DK), 0)
    causal_bias = [jnp.where(row >= s, 0.0, NEG) for s in range(R)]
    nt = (((1,), (1,)), ((), ()))
    tn = (((0,), (0,)), ((), ()))

    def sub(i):
        r0 = i * T
        rg = [i * T + g * R for g in range(groups)]
        for h in range(H):
            q_g = [q_s[h, pl.ds(rg[g], R), :] for g in range(groups)]
            b_g = [b_s[h, pl.ds(rg[g], R), :] for g in range(groups)]
            acc = [jnp.zeros((R, HG_DV), F32) for _ in range(groups)]
            for s in range(T):
                ks = k_s[h, pl.ds(r0 + s, 1), :]
                bs = b_s[h, pl.ds(r0 + s, 1), :]
                vs = v_s[h, pl.ds(r0 + s, 1), :]
                for g in range(s // R, groups):
                    diff = b_g[g] - bs
                    if g == s // R:
                        diff = diff + causal_bias[s % R]
                    col = jnp.sum(q_g[g] * ks * jnp.exp2(diff), axis=-1, keepdims=True)
                    acc[g] = acc[g] + col * vs
            st = st_ref[h]
            inter = lax.dot_general(qd_s[h, pl.ds(r0, T), :], st, nt, preferred_element_type=F32)
            for g in range(groups):
                o_s[h, pl.ds(rg[g], R), :] = acc[g] + inter[g * R:(g + 1) * R]
            upd = lax.dot_general(v_s[h, pl.ds(r0, T), :], kd_s[h, pl.ds(r0, T), :], tn,
                                  preferred_element_type=F32)
            st_ref[h] = st * es_s[h, pl.ds(r0, 1), :] + upd

    for i in range(lc // T):
        sub(i)

    gain = gain_ref[...]
    for h in range(H):
        cs = slice(h * HG_DV, (h + 1) * HG_DV)
        hg = hg_ref[0, :, cs]
        o_ref[0, :, cs] = _rms(o_s[h], gain) * (hg * jax.nn.sigmoid(hg))


def _hgrn(zh, lb_table, gain, layer, lc):
    B, S, _ = zh.shape
    W = HG_HEADS * HG_DK
    blk = lambda part: pl.BlockSpec((1, lc, W), lambda b, c: (b, c, part))
    scr = lambda: pltpu.VMEM((HG_HEADS, lc, HG_DK), F32)
    return pl.pallas_call(
        functools.partial(_hgrn_body, lc=lc, layer=layer),
        grid=(B, S // lc),
        in_specs=[
            blk(0), blk(1), blk(2), blk(3),
            pl.BlockSpec(lb_table.shape, lambda b, c: (0, 0)),
            pl.BlockSpec((1, HG_DV), lambda b, c: (0, 0)),
        ],
        out_specs=pl.BlockSpec((1, lc, W), lambda b, c: (b, c, 0)),
        out_shape=jax.ShapeDtypeStruct((B, S, W), F32),
        scratch_shapes=[pltpu.VMEM((HG_HEADS, HG_DV, HG_DK), F32)] + [scr() for _ in range(8)],
        compiler_params=_params("parallel", "arbitrary"),
        name="hgrn2",
    )(zh, zh, zh, zh, lb_table, gain)


def _mix_out_body(a_ref, r_ref, x_ref, gm_ref, wa_ref, wr_ref, gp_ref, o_ref):
    an = _rms(a_ref[...], gm_ref[...]).astype(BF16)
    y = jnp.dot(an, wa_ref[...], preferred_element_type=F32)
    y = y + jnp.dot(r_ref[...].astype(BF16), wr_ref[...], preferred_element_type=F32)
    o_ref[...] = x_ref[...] + _rms(y, gp_ref[...])


def _mix_out(a2d, r2d, x2d, gm, wa, wr, gp, tm):
    N, D = x2d.shape
    return pl.pallas_call(
        _mix_out_body,
        grid=(N // tm,),
        in_specs=[
            pl.BlockSpec((tm, a2d.shape[1]), lambda i: (i, 0)),
            pl.BlockSpec((tm, r2d.shape[1]), lambda i: (i, 0)),
            pl.BlockSpec((tm, D), lambda i: (i, 0)),
            _const_spec(gm.shape), _const_spec(wa.shape), _const_spec(wr.shape), _const_spec(gp.shape),
        ],
        out_specs=pl.BlockSpec((tm, D), lambda i: (i, 0)),
        out_shape=jax.ShapeDtypeStruct((N, D), F32),
        compiler_params=_params("parallel"),
        name="mix_out",
    )(a2d, r2d, x2d, gm, wa, wr, gp)


def _xattn_body(h_ref, gpre_ref, wq_ref, kv_ref, wo_ref, gpost_ref, o_ref, oh_ref, *, q_scale):
    h = h_ref[...]
    D = h.shape[-1]
    dh = D // X_HEADS
    xn = _rms(h, gpre_ref[...]).astype(BF16)
    qx = (jnp.dot(xn, wq_ref[...], preferred_element_type=F32) * q_scale).astype(BF16)
    for hd in range(X_HEADS):
        qh = qx[:, hd * dh:(hd + 1) * dh]
        kh = kv_ref[0, :, hd * dh:(hd + 1) * dh]
        vh = kv_ref[0, :, D + hd * dh:D + (hd + 1) * dh]
        s = lax.dot_general(qh, kh, (((1,), (1,)), ((), ())), preferred_element_type=F32)
        p = jnp.exp2(s - jnp.max(s, axis=-1, keepdims=True))
        l = jnp.sum(p, axis=-1, keepdims=True)
        oh = jnp.dot(p.astype(BF16), vh, preferred_element_type=F32) / l
        oh_ref[:, hd * dh:(hd + 1) * dh] = oh.astype(BF16)
    ox = jnp.dot(oh_ref[...], wo_ref[...], preferred_element_type=F32)
    o_ref[...] = h + _rms(ox, gpost_ref[...])


def _xattn(h2d, gpre, wq, kvm, wo, gpost, tm, tiles_per_batch):
    N, D = h2d.shape
    M = kvm.shape[1]
    q_scale = ((D // X_HEADS) ** -0.5) * LOG2E
    return pl.pallas_call(
        functools.partial(_xattn_body, q_scale=q_scale),
        grid=(N // tm,),
        in_specs=[
            pl.BlockSpec((tm, D), lambda i: (i, 0)),
            _const_spec(gpre.shape), _const_spec(wq.shape),
            pl.BlockSpec((1, M, 2 * D), lambda i: (i // tiles_per_batch, 0, 0)),
            _const_spec(wo.shape), _const_spec(gpost.shape),
        ],
        out_specs=pl.BlockSpec((tm, D), lambda i: (i, 0)),
        out_shape=jax.ShapeDtypeStruct((N, D), F32),
        scratch_shapes=[pltpu.VMEM((tm, D), BF16)],
        compiler_params=_params("parallel"),
        name="xattn",
    )(h2d, gpre, wq, kvm, wo, gpost)


def _gelu_tanh(x):
    return x * (0.5 * (1.0 + jnp.tanh(math.sqrt(2.0 / math.pi) * (x + 0.044715 * (x * x * x)))))


def _ffn_up_body(h_ref, gpre_ref, wg_ref, wv_ref, cwg_ref, cwv_ref, cbg_ref, cbv_ref,
                 act_ref, xn_ref, cg_ref, cv_ref, bg0_ref, bv0_ref, bg1_ref, bv1_ref,
                 *, tm, nj, tiles_per_batch):
    i = pl.program_id(0)
    j = pl.program_id(1)
    first = (i % tiles_per_batch) == 0
    slots = ((bg0_ref, bv0_ref), (bg1_ref, bv1_ref))

    def project(slot):
        for w_ref, carry_ref, buf_ref in zip((wg_ref, wv_ref), (cg_ref, cv_ref), slots[slot]):
            u = jnp.dot(xn_ref[...], w_ref[...], preferred_element_type=F32)
            buf_ref[0:CONV_PAD, :] = jnp.where(first, 0.0, carry_ref[j])
            buf_ref[CONV_PAD:CONV_PAD + tm, :] = u
            carry_ref[j] = u[tm - CONV_PAD:, :]

    def conv(buf_ref, cw, cb, r):
        base = CONV_PAD + r
        return (cw[0:1, :] * buf_ref[base - 2:base - 2 + FF_CHUNK, :]
                + cw[1:2, :] * buf_ref[base - 1:base - 1 + FF_CHUNK, :]
                + cw[2:3, :] * buf_ref[base:base + FF_CHUNK, :] + cb)

    def finish(slot):
        bg_ref, bv_ref = slots[slot]
        cwg = cwg_ref[...]
        cwv = cwv_ref[...]
        cbg = cbg_ref[...]
        cbv = cbv_ref[...]
        for r in range(0, tm, FF_CHUNK):
            gate = conv(bg_ref, cwg, cbg, r)
            val = conv(bv_ref, cwv, cbv, r)
            act_ref[r:r + FF_CHUNK, :] = (_gelu_tanh(gate) * val).astype(BF16)

    @pl.when(j == 0)
    def _():
        @pl.when(i == 0)
        def _():
            cg_ref[...] = jnp.zeros(cg_ref.shape, F32)
            cv_ref[...] = jnp.zeros(cv_ref.shape, F32)

        xn_ref[...] = _rms(h_ref[...], gpre_ref[...]).astype(BF16)
        project(0)

    for parity in (0, 1):
        @pl.when((j > 0) & (j < nj) & (j % 2 == parity))
        def _():
            finish(1 - parity)
            project(parity)

    @pl.when(j == nj)
    def _():
        finish((nj - 1) % 2)


def _ffn_up(h2d, gpre, wg, wv, cwg, cwv, cbg, cbv, tm, tf, tiles_per_batch):
    N, D = h2d.shape
    F = wg.shape[1]
    nj = F // tf
    cur = lambda i, j: (0, jnp.minimum(j, nj - 1))
    prev = lambda i, j: (0, jnp.maximum(j - 1, 0))
    buf = lambda: pltpu.VMEM((CONV_PAD + tm, tf), F32)
    return pl.pallas_call(
        functools.partial(_ffn_up_body, tm=tm, nj=nj, tiles_per_batch=tiles_per_batch),
        grid=(N // tm, nj + 1),
        in_specs=[
            pl.BlockSpec((tm, D), lambda i, j: (i, 0)),
            _const_spec(gpre.shape),
            pl.BlockSpec((D, tf), cur),
            pl.BlockSpec((D, tf), cur),
            pl.BlockSpec((CONV_WIDTH, tf), prev), pl.BlockSpec((CONV_WIDTH, tf), prev),
            pl.BlockSpec((1, tf), prev), pl.BlockSpec((1, tf), prev),
        ],
        out_specs=pl.BlockSpec((tm, tf), lambda i, j: (i, jnp.maximum(j - 1, 0))),
        out_shape=jax.ShapeDtypeStruct((N, F), BF16),
        scratch_shapes=[
            pltpu.VMEM((tm, D), BF16),
            pltpu.VMEM((nj, CONV_PAD, tf), F32),
            pltpu.VMEM((nj, CONV_PAD, tf), F32),
            buf(), buf(), buf(), buf(),
        ],
        compiler_params=_params("arbitrary", "arbitrary"),
        name="ffn_up",
    )(h2d, gpre, wg, wv, cwg, cwv, cbg, cbv)


def _ffn_down_body(act_ref, h_ref, wd_ref, gpost_ref, o_ref):
    y = jnp.dot(act_ref[...], wd_ref[...], preferred_element_type=F32)
    o_ref[...] = h_ref[...] + _rms(y, gpost_ref[...])


def _ffn_down(act, h2d, wd, gpost, tm):
    N, D = h2d.shape
    F = act.shape[1]
    return pl.pallas_call(
        _ffn_down_body,
        grid=(N // tm,),
        in_specs=[
            pl.BlockSpec((tm, F), lambda i: (i, 0)),
            pl.BlockSpec((tm, D), lambda i: (i, 0)),
            _const_spec(wd.shape), _const_spec(gpost.shape),
        ],
        out_specs=pl.BlockSpec((tm, D), lambda i: (i, 0)),
        out_shape=jax.ShapeDtypeStruct((N, D), F32),
        compiler_params=_params("parallel"),
        name="ffn_down",
    )(act, h2d, wd, gpost)


def kernel(x, mem, w_in, q_norm, w_uq, kv_norm, w_ukv, mla_out_norm, hgrn_lb, hgrn_out_norm, w_out,
           ln_mix_pre, ln_mix_post, ln_x_pre, ln_x_post, mem_norm, w_xq, w_xk, w_xv, w_xo,
           ln_ffn_pre, ln_ffn_post, w_up, conv_w, conv_b, w_down):
    B, S, D = x.shape
    M = mem.shape[1]
    depth = w_in.shape[0]
    N = B * S
    tm = min(ROW_TILE, S)
    blk = min(ATT_BLK, S)
    bq = min(ATT_QBLK, S)
    lc = min(HG_BLK, S)
    assert S % tm == 0 and S % blk == 0 and S % bq == 0 and bq % blk == 0 and blk % CHUNK == 0
    assert S % lc == 0 and lc % HG_SUB == 0
    tiles_per_batch = S // tm
    d_ff = w_down.shape[1]
    tf = FF_TILE
    tmf = min(FF_ROWS, S)
    assert d_ff % tf == 0 and S % tmf == 0 and tmf % FF_CHUNK == 0
    n_mla_in = MLA_Q_RANK + MLA_KV_RANK + MLA_ROPE

    pos = jnp.arange(S, dtype=F32)
    inv_freq = 1.0 / (ROPE_THETA ** (jnp.arange(0, MLA_ROPE, 2, dtype=F32) / MLA_ROPE))
    ang = pos[:, None] * inv_freq[None, :]
    cos, sin = jnp.cos(ang), jnp.sin(ang)
    cosT, sinT = cos.T, sin.T

    row = lambda v: v.reshape(1, -1).astype(F32)
    h = x
    for l in range(depth):
        w_mla = w_in[l][:, :n_mla_in].astype(BF16)
        w_hg = w_in[l][:, n_mla_in:].astype(BF16)
        w_uqT = w_uq[l].T.astype(BF16)
        w_ukv_h = w_ukv[l].reshape(MLA_KV_RANK, MLA_HEADS, MLA_NOPE + MLA_V)
        w_uk = w_ukv_h[:, :, :MLA_NOPE].reshape(MLA_KV_RANK, MLA_HEADS * MLA_NOPE).astype(BF16)
        w_uvT = w_ukv_h[:, :, MLA_NOPE:].reshape(MLA_KV_RANK, MLA_HEADS * MLA_V).T.astype(BF16)
        w_out_a = w_out[l][:MLA_WIDTH].astype(BF16)
        w_out_r = w_out[l][MLA_WIDTH:].astype(BF16)
        w_xkv = jnp.concatenate([w_xk[l], w_xv[l]], axis=1).astype(BF16)
        w_up_g = w_up[l][:, :d_ff].astype(BF16)
        w_up_v = w_up[l][:, d_ff:].astype(BF16)

        qT, k, vT = _mla_prep(h, row(ln_mix_pre[l]), w_mla, row(q_norm[l]), w_uqT, row(kv_norm[l]),
                              w_uk, w_uvT, cos, sin, cosT, sinT, blk)
        a = _attention(qT, k, vT, bq, blk)
        zh = _norm_matmul(h.reshape(N, D), row(ln_mix_pre[l]), w_hg, F32, tm, w_hg.shape[1], "hgrn_in")
        r = _hgrn(zh.reshape(B, S, -1), hgrn_lb.astype(F32), row(hgrn_out_norm[l]), l, lc)
        h2d = _mix_out(a.reshape(N, -1), r.reshape(N, -1), h.reshape(N, D), row(mla_out_norm[l]),
                       w_out_a, w_out_r, row(ln_mix_post[l]), tm)

        kvm = _norm_matmul(mem.reshape(B * M, D), row(mem_norm[l]), w_xkv, BF16, min(256, B * M), 1024, "mem_kv")
        h2d = _xattn(h2d, row(ln_x_pre[l]), w_xq[l].astype(BF16), kvm.reshape(B, M, 2 * D),
                     w_xo[l].astype(BF16), row(ln_x_post[l]), tm, tiles_per_batch)

        act = _ffn_up(h2d, row(ln_ffn_pre[l]), w_up_g, w_up_v, conv_w[l][:, :d_ff], conv_w[l][:, d_ff:],
                      row(conv_b[l][:d_ff]), row(conv_b[l][d_ff:]), tmf, tf, S // tmf)
        h2d = _ffn_down(act, h2d, w_down[l].astype(BF16), row(ln_ffn_post[l]), min(FF_DOWN_ROWS, S))
        h = h2d.reshape(B, S, D)
    return h
```

```python
import functools
import math

import jax
import jax.numpy as jnp
from jax import lax
from jax.experimental import pallas as pl
from jax.experimental.pallas import tpu as pltpu

F32 = jnp.float32
BF16 = jnp.bfloat16

EPS = 1e-6
CHUNK = 64
MLA_V = 128
MLA_NOPE = 128
MLA_ROPE = 64
MLA_HEADS = 8
MLA_QK = MLA_NOPE + MLA_ROPE
MLA_KPAD = 256
MLA_Q_RANK = 512
MLA_KV_RANK = 256
ROPE_THETA = 10000.0
HG_DK = 128
HG_DV = 128
HG_HEADS = 8
HG_WIDTH = HG_HEADS * HG_DV
MLA_WIDTH = MLA_HEADS * MLA_V
X_HEADS = 4
CONV_WIDTH = 3

LOG2E = 1.4426950408889634
NEG = -1e30

VMEM_LIMIT_BYTES = 56 * 1024 * 1024

ROW_TILE = 512
ATT_BLK = 1024
ATT_QBLK = 2048
ATT_MIN_MASS = 2.0 ** -80
HG_BLK = 256
HG_SUB = 16
SUBLANES = 8
FF_TILE = 512
FF_ROWS = 1024
FF_CHUNK = 64
FF_DOWN_ROWS = 256
CONV_PAD = 8


def _params(*sem):
    return pltpu.CompilerParams(dimension_semantics=sem, vmem_limit_bytes=VMEM_LIMIT_BYTES)


def _rms(xf, g):
    return xf * lax.rsqrt(jnp.mean(xf * xf, axis=-1, keepdims=True) + EPS) * g


def _const_spec(shape):
    nd = len(shape)
    return pl.BlockSpec(shape, lambda *_: (0,) * nd, pipeline_mode=pl.Buffered(1))


def _mla_prep_body(x_ref, g_ref, wm_ref, qn_ref, wuqT_ref, kvn_ref, wuk_ref, wuvT_ref,
                   cos_ref, sin_ref, cosT_ref, sinT_ref, qT_ref, k_ref, vT_ref, *, q_scale):
    xn = _rms(x_ref[0], g_ref[...]).astype(BF16)
    z = jnp.dot(xn, wm_ref[...], preferred_element_type=F32)
    cq = _rms(z[:, :MLA_Q_RANK], qn_ref[...]).astype(BF16)
    ckv = _rms(z[:, MLA_Q_RANK:MLA_Q_RANK + MLA_KV_RANK], kvn_ref[...]).astype(BF16)
    kr = z[:, MLA_Q_RANK + MLA_KV_RANK:]

    nt = (((1,), (1,)), ((), ()))
    qT = lax.dot_general(wuqT_ref[...], cq, nt, preferred_element_type=F32)
    cosT = cosT_ref[...]
    sinT = sinT_ref[...]
    half = MLA_ROPE // 2
    for h in range(MLA_HEADS):
        base = h * MLA_QK
        x1 = qT[base + MLA_NOPE:base + MLA_NOPE + half]
        x2 = qT[base + MLA_NOPE + half:base + MLA_QK]
        qT_ref[0, base:base + MLA_NOPE, :] = (qT[base:base + MLA_NOPE] * q_scale).astype(BF16)
        qT_ref[0, base + MLA_NOPE:base + MLA_NOPE + half, :] = ((x1 * cosT - x2 * sinT) * q_scale).astype(BF16)
        qT_ref[0, base + MLA_NOPE + half:base + MLA_QK, :] = ((x2 * cosT + x1 * sinT) * q_scale).astype(BF16)

    kn = jnp.dot(ckv, wuk_ref[...], preferred_element_type=F32)
    cos = cos_ref[...]
    sin = sin_ref[...]
    k1 = kr[:, :half]
    k2 = kr[:, half:]
    krope = jnp.concatenate([k1 * cos - k2 * sin, k2 * cos + k1 * sin], axis=-1).astype(BF16)
    lane = lax.broadcasted_iota(jnp.int32, (kr.shape[0], MLA_KPAD - MLA_QK), 1)
    one_hot = jnp.where(lane == 0, 1.0, 0.0).astype(BF16)
    for h in range(MLA_HEADS):
        k_ref[0, h, :, 0:MLA_NOPE] = kn[:, h * MLA_NOPE:(h + 1) * MLA_NOPE].astype(BF16)
        k_ref[0, h, :, MLA_NOPE:MLA_QK] = krope
        k_ref[0, h, :, MLA_QK:MLA_KPAD] = one_hot

    vT = lax.dot_general(wuvT_ref[...], ckv, nt, preferred_element_type=F32)
    for h in range(MLA_HEADS):
        vT_ref[0, h, 0] = vT[h * MLA_V:(h + 1) * MLA_V].astype(BF16)


def _mla_prep(x, g, wm, qn, wuqT, kvn, wuk, wuvT, cos, sin, cosT, sinT, blk):
    B, S, D = x.shape
    nb = S // blk
    half = MLA_ROPE // 2
    q_scale = (MLA_QK ** -0.5) * LOG2E
    return pl.pallas_call(
        functools.partial(_mla_prep_body, q_scale=q_scale),
        grid=(B, nb),
        in_specs=[
            pl.BlockSpec((1, blk, D), lambda b, i: (b, i, 0)),
            _const_spec(g.shape), _const_spec(wm.shape), _const_spec(qn.shape), _const_spec(wuqT.shape),
            _const_spec(kvn.shape), _const_spec(wuk.shape), _const_spec(wuvT.shape),
            pl.BlockSpec((blk, half), lambda b, i: (i, 0)),
            pl.BlockSpec((blk, half), lambda b, i: (i, 0)),
            pl.BlockSpec((half, blk), lambda b, i: (0, i)),
            pl.BlockSpec((half, blk), lambda b, i: (0, i)),
        ],
        out_specs=[
            pl.BlockSpec((1, MLA_HEADS * MLA_QK, blk), lambda b, i: (b, 0, i)),
            pl.BlockSpec((1, MLA_HEADS, blk, MLA_KPAD), lambda b, i: (b, 0, i, 0)),
            pl.BlockSpec((1, MLA_HEADS, 1, MLA_V, blk), lambda b, i: (b, 0, i, 0, 0)),
        ],
        out_shape=[
            jax.ShapeDtypeStruct((B, MLA_HEADS * MLA_QK, S), BF16),
            jax.ShapeDtypeStruct((B, MLA_HEADS, S, MLA_KPAD), BF16),
            jax.ShapeDtypeStruct((B, MLA_HEADS, nb, MLA_V, blk), BF16),
        ],
        compiler_params=_params("parallel", "parallel"),
        name="mla_prep",
    )(x, g, wm, qn, wuqT, kvn, wuk, wuvT, cos, sin, cosT, sinT)


def _attn_body(qT_ref, k_ref, vT_ref, o_ref, qa_ref, kn_ref, m_ref, l_ref, acc_ref, *, bq, bk):
    i = pl.program_id(2)
    ratio = bq // bk
    qT = qT_ref[0]
    qa_ref[0:MLA_QK, :] = qT

    def set_offset_row(offset):
        row = lax.broadcasted_iota(jnp.int32, (MLA_KPAD - MLA_QK, bq), 0)
        qa_ref[MLA_QK:, :] = jnp.where(row == 0, offset, 0.0).astype(BF16)

    @pl.when(i == 0)
    def _():
        def kmax(c, best):
            kc = k_ref[0, 0, pl.ds(pl.multiple_of(c * bk, bk), bk), :].astype(F32)
            sq = jnp.sum(kc * kc, axis=-1, keepdims=True)
            return jnp.maximum(best, jnp.max(sq, axis=0, keepdims=True))

        best = lax.fori_loop(0, k_ref.shape[2] // bk, kmax, jnp.zeros((1, 1), F32))
        kn_ref[...] = jnp.broadcast_to(jnp.sqrt(best), kn_ref.shape)

    def scores(j, c0, masked):
        kblk = k_ref[0, 0, pl.ds(pl.multiple_of(j * bk, bk), bk), :]
        s = jnp.dot(kblk, qa_ref[:, c0:], preferred_element_type=F32)
        if masked:
            key_chunk = lax.broadcasted_iota(jnp.int32, s.shape, 0) // CHUNK
            qry_chunk = lax.broadcasted_iota(jnp.int32, s.shape, 1) // CHUNK
            s = jnp.where(key_chunk <= qry_chunk, s, NEG)
        return s

    def sweep(update):
        l_ref[...] = jnp.zeros(l_ref.shape, F32)
        acc_ref[...] = jnp.zeros(acc_ref.shape, F32)

        def body(jj, carry):
            for u in range(ratio):
                update(jj * ratio + u, 0, False)
            return carry

        lax.fori_loop(0, i, body, 0)
        for d in range(ratio):
            update(i * ratio + d, d * bk, True)

    def fixed_update(j, c0, masked):
        p = jnp.exp2(scores(j, c0, masked))
        l_ref[:, c0:] += jnp.sum(p, axis=0, keepdims=True)
        acc_ref[:, c0:] += jnp.dot(vT_ref[0, 0, j], p.astype(BF16), preferred_element_type=F32)

    qf = qT.astype(F32)
    set_offset_row(-(jnp.sqrt(jnp.sum(qf * qf, axis=0, keepdims=True)) * kn_ref[...] + 1.0))
    sweep(fixed_update)

    def online_update(j, c0, masked):
        s = scores(j, c0, masked)
        m_prev = m_ref[:, c0:]
        m_new = jnp.maximum(m_prev, jnp.max(s, axis=0, keepdims=True))
        alpha = jnp.exp2(m_prev - m_new)
        p = jnp.exp2(s - m_new)
        l_ref[:, c0:] = alpha * l_ref[:, c0:] + jnp.sum(p, axis=0, keepdims=True)
        acc_ref[:, c0:] = alpha * acc_ref[:, c0:] + jnp.dot(vT_ref[0, 0, j], p.astype(BF16),
                                                            preferred_element_type=F32)
        m_ref[:, c0:] = m_new

    @pl.when(jnp.logical_not(jnp.min(l_ref[...]) >= ATT_MIN_MASS))
    def _():
        set_offset_row(jnp.zeros((1, bq), F32))
        m_ref[...] = jnp.full(m_ref.shape, NEG, F32)
        sweep(online_update)

    o_ref[0] = (acc_ref[...] / l_ref[...]).T


def _attention(qT, k, vT, bq, bk):
    B, H, S, _ = k.shape
    return pl.pallas_call(
        functools.partial(_attn_body, bq=bq, bk=bk),
        grid=(B, H, S // bq),
        in_specs=[
            pl.BlockSpec((1, MLA_QK, bq), lambda b, h, i: (b, h, i)),
            pl.BlockSpec((1, 1, S, MLA_KPAD), lambda b, h, i: (b, h, 0, 0)),
            pl.BlockSpec((1, 1, S // bk, MLA_V, bk), lambda b, h, i: (b, h, 0, 0, 0)),
        ],
        out_specs=pl.BlockSpec((1, bq, MLA_V), lambda b, h, i: (b, i, h)),
        out_shape=jax.ShapeDtypeStruct((B, S, H * MLA_V), F32),
        scratch_shapes=[pltpu.VMEM((MLA_KPAD, bq), BF16),
                        pltpu.VMEM((1, bq), F32), pltpu.VMEM((1, bq), F32), pltpu.VMEM((1, bq), F32),
                        pltpu.VMEM((MLA_V, bq), F32)],
        compiler_params=_params("parallel", "parallel", "arbitrary"),
        name="mla_attention",
    )(qT, k, vT)


def _norm_matmul_body(x_ref, g_ref, w_ref, o_ref, xn_ref):
    @pl.when(pl.program_id(1) == 0)
    def _():
        xn_ref[...] = _rms(x_ref[...], g_ref[...]).astype(BF16)

    o_ref[...] = jnp.dot(xn_ref[...], w_ref[...], preferred_element_type=F32).astype(o_ref.dtype)


def _norm_matmul(x2d, g, w, out_dtype, tm, tn, name):
    N, D = x2d.shape
    M = w.shape[1]
    w_spec = _const_spec(w.shape) if tn == M else pl.BlockSpec((D, tn), lambda i, j: (0, j))
    return pl.pallas_call(
        _norm_matmul_body,
        grid=(N // tm, M // tn),
        in_specs=[
            pl.BlockSpec((tm, D), lambda i, j: (i, 0)),
            _const_spec(g.shape),
            w_spec,
        ],
        out_specs=pl.BlockSpec((tm, tn), lambda i, j: (i, j)),
        out_shape=jax.ShapeDtypeStruct((N, M), out_dtype),
        scratch_shapes=[pltpu.VMEM((tm, D), BF16)],
        compiler_params=_params("parallel", "arbitrary"),
        name=name,
    )(x2d, g, w)


def _group_scan(x, row_in_group, group, reverse):
    n = x.shape[0]
    sh = 1
    while sh < group:
        if reverse:
            x = x + jnp.where(row_in_group < group - sh, pltpu.roll(x, n - sh, axis=0), 0.0)
        else:
            x = x + jnp.where(row_in_group >= sh, pltpu.roll(x, sh, axis=0), 0.0)
        sh *= 2
    return x


def _hgrn_body(hq_ref, hf_ref, hi_ref, hg_ref, lbp_ref, gain_ref, o_ref,
               st_ref, q_s, k_s, b_s, v_s, qd_s, kd_s, es_s, o_s, *, lc, layer):
    T = HG_SUB
    H = HG_HEADS

    @pl.when(pl.program_id(1) == 0)
    def _():
        st_ref[...] = jnp.zeros(st_ref.shape, F32)

    lbp = lbp_ref[...]
    e = jnp.exp(lbp - jnp.max(lbp, axis=0, keepdims=True))
    prob = e / jnp.sum(e, axis=0, keepdims=True)
    lb = jnp.sum(prob[:layer + 1], axis=0, keepdims=True)

    hq = hq_ref[0]
    q = hq * jax.nn.sigmoid(hq)
    f = lb + (1.0 - lb) * jax.nn.sigmoid(hf_ref[0])
    k = 1.0 - f
    logf = jnp.log(f)

    rig = lax.broadcasted_iota(jnp.int32, logf.shape, 0) % T
    b = _group_scan(logf, rig, T, reverse=False)
    suffix = _group_scan(logf, rig, T, reverse=True)
    staged = ((q_s, q), (k_s, k), (b_s, b * LOG2E), (v_s, hi_ref[0]), (qd_s, q * jnp.exp(b)),
              (kd_s, k * jnp.exp(suffix - logf)),
              (es_s, jnp.exp(suffix)))
    for ref, val in staged:
        for h in range(H):
            ref[h] = val[:, h * HG_DK:(h + 1) * HG_DK]

    R = SUBLANES
    groups = T // R
    row = lax.broadcasted_iota(jnp.int32, (R, HG_DK), 0)
    causal_bias = [jnp.where(row >= s, 0.0, NEG) for s in range(R)]
    nt = (((1,), (1,)), ((), ()))
    tn = (((0,), (0,)), ((), ()))

    def sub(i):
        r0 = i * T
        rg = [i * T + g * R for g in range(groups)]
        for h in range(H):
            q_g = [q_s[h, pl.ds(rg[g], R), :] for g in range(groups)]
            b_g = [b_s[h, pl.ds(rg[g], R), :] for g in range(groups)]
            acc = [jnp.zeros((R, HG_DV), F32) for _ in range(groups)]
            for s in range(T):
                ks = k_s[h, pl.ds(r0 + s, 1), :]
                bs = b_s[h, pl.ds(r0 + s, 1), :]
                vs = v_s[h, pl.ds(r0 + s, 1), :]
                for g in range(s // R, groups):
                    diff = b_g[g] - bs
                    if g == s // R:
                        diff = diff + causal_bias[s % R]
                    col = jnp.sum(q_g[g] * ks * jnp.exp2(diff), axis=-1, keepdims=True)
                    acc[g] = acc[g] + col * vs
            st = st_ref[h]
            inter = lax.dot_general(qd_s[h, pl.ds(r0, T), :], st, nt, preferred_element_type=F32)
            for g in range(groups):
                o_s[h, pl.ds(rg[g], R), :] = acc[g] + inter[g * R:(g + 1) * R]
            upd = lax.dot_general(v_s[h, pl.ds(r0, T), :], kd_s[h, pl.ds(r0, T), :], tn,
                                  preferred_element_type=F32)
            st_ref[h] = st * es_s[h, pl.ds(r0, 1), :] + upd

    for i in range(lc // T):
        sub(i)

    gain = gain_ref[...]
    for h in range(H):
        cs = slice(h * HG_DV, (h + 1) * HG_DV)
        hg = hg_ref[0, :, cs]
        o_ref[0, :, cs] = _rms(o_s[h], gain) * (hg * jax.nn.sigmoid(hg))


def _hgrn(zh, lb_table, gain, layer, lc):
    B, S, _ = zh.shape
    W = HG_HEADS * HG_DK
    blk = lambda part: pl.BlockSpec((1, lc, W), lambda b, c: (b, c, part))
    scr = lambda: pltpu.VMEM((HG_HEADS, lc, HG_DK), F32)
    return pl.pallas_call(
        functools.partial(_hgrn_body, lc=lc, layer=layer),
        grid=(B, S // lc),
        in_specs=[
            blk(0), blk(1), blk(2), blk(3),
            pl.BlockSpec(lb_table.shape, lambda b, c: (0, 0)),
            pl.BlockSpec((1, HG_DV), lambda b, c: (0, 0)),
        ],
        out_specs=pl.BlockSpec((1, lc, W), lambda b, c: (b, c, 0)),
        out_shape=jax.ShapeDtypeStruct((B, S, W), F32),
        scratch_shapes=[pltpu.VMEM((HG_HEADS, HG_DV, HG_DK), F32)] + [scr() for _ in range(8)],
        compiler_params=_params("parallel", "arbitrary"),
        name="hgrn2",
    )(zh, zh, zh, zh, lb_table, gain)


def _mix_out_body(a_ref, r_ref, x_ref, gm_ref, wa_ref, wr_ref, gp_ref, o_ref):
    an = _rms(a_ref[...], gm_ref[...]).astype(BF16)
    y = jnp.dot(an, wa_ref[...], preferred_element_type=F32)
    y = y + jnp.dot(r_ref[...].astype(BF16), wr_ref[...], preferred_element_type=F32)
    o_ref[...] = x_ref[...] + _rms(y, gp_ref[...])


def _mix_out(a2d, r2d, x2d, gm, wa, wr, gp, tm):
    N, D = x2d.shape
    return pl.pallas_call(
        _mix_out_body,
        grid=(N // tm,),
        in_specs=[
            pl.BlockSpec((tm, a2d.shape[1]), lambda i: (i, 0)),
            pl.BlockSpec((tm, r2d.shape[1]), lambda i: (i, 0)),
            pl.BlockSpec((tm, D), lambda i: (i, 0)),
            _const_spec(gm.shape), _const_spec(wa.shape), _const_spec(wr.shape), _const_spec(gp.shape),
        ],
        out_specs=pl.BlockSpec((tm, D), lambda i: (i, 0)),
        out_shape=jax.ShapeDtypeStruct((N, D), F32),
        compiler_params=_params("parallel"),
        name="mix_out",
    )(a2d, r2d, x2d, gm, wa, wr, gp)


def _xattn_body(h_ref, gpre_ref, wq_ref, kv_ref, wo_ref, gpost_ref, o_ref, oh_ref, *, q_scale):
    h = h_ref[...]
    D = h.shape[-1]
    dh = D // X_HEADS
    xn = _rms(h, gpre_ref[...]).astype(BF16)
    qx = (jnp.dot(xn, wq_ref[...], preferred_element_type=F32) * q_scale).astype(BF16)
    for hd in range(X_HEADS):
        qh = qx[:, hd * dh:(hd + 1) * dh]
        kh = kv_ref[0, :, hd * dh:(hd + 1) * dh]
        vh = kv_ref[0, :, D + hd * dh:D + (hd + 1) * dh]
        s = lax.dot_general(qh, kh, (((1,), (1,)), ((), ())), preferred_element_type=F32)
        p = jnp.exp2(s - jnp.max(s, axis=-1, keepdims=True))
        l = jnp.sum(p, axis=-1, keepdims=True)
        oh = jnp.dot(p.astype(BF16), vh, preferred_element_type=F32) / l
        oh_ref[:, hd * dh:(hd + 1) * dh] = oh.astype(BF16)
    ox = jnp.dot(oh_ref[...], wo_ref[...], preferred_element_type=F32)
    o_ref[...] = h + _rms(ox, gpost_ref[...])


def _xattn(h2d, gpre, wq, kvm, wo, gpost, tm, tiles_per_batch):
    N, D = h2d.shape
    M = kvm.shape[1]
    q_scale = ((D // X_HEADS) ** -0.5) * LOG2E
    return pl.pallas_call(
        functools.partial(_xattn_body, q_scale=q_scale),
        grid=(N // tm,),
        in_specs=[
            pl.BlockSpec((tm, D), lambda i: (i, 0)),
            _const_spec(gpre.shape), _const_spec(wq.shape),
            pl.BlockSpec((1, M, 2 * D), lambda i: (i // tiles_per_batch, 0, 0)),
            _const_spec(wo.shape), _const_spec(gpost.shape),
        ],
        out_specs=pl.BlockSpec((tm, D), lambda i: (i, 0)),
        out_shape=jax.ShapeDtypeStruct((N, D), F32),
        scratch_shapes=[pltpu.VMEM((tm, D), BF16)],
        compiler_params=_params("parallel"),
        name="xattn",
    )(h2d, gpre, wq, kvm, wo, gpost)


def _gelu_tanh(x):
    return x * (0.5 * (1.0 + jnp.tanh(math.sqrt(2.0 / math.pi) * (x + 0.044715 * (x * x * x)))))


def _ffn_up_body(h_ref, gpre_ref, wg_ref, wv_ref, cwg_ref, cwv_ref, cbg_ref, cbv_ref,
                 act_ref, xn_ref, cg_ref, cv_ref, bg0_ref, bv0_ref, bg1_ref, bv1_ref,
                 *, tm, nj, tiles_per_batch):
    i = pl.program_id(0)
    j = pl.program_id(1)
    first = (i % tiles_per_batch) == 0
    slots = ((bg0_ref, bv0_ref), (bg1_ref, bv1_ref))

    def project(slot):
        for w_ref, carry_ref, buf_ref in zip((wg_ref, wv_ref), (cg_ref, cv_ref), slots[slot]):
            u = jnp.dot(xn_ref[...], w_ref[...], preferred_element_type=F32)
            buf_ref[0:CONV_PAD, :] = jnp.where(first, 0.0, carry_ref[j])
            buf_ref[CONV_PAD:CONV_PAD + tm, :] = u
            carry_ref[j] = u[tm - CONV_PAD:, :]

    def conv(buf_ref, cw, cb, r):
        base = CONV_PAD + r
        return (cw[0:1, :] * buf_ref[base - 2:base - 2 + FF_CHUNK, :]
                + cw[1:2, :] * buf_ref[base - 1:base - 1 + FF_CHUNK, :]
                + cw[2:3, :] * buf_ref[base:base + FF_CHUNK, :] + cb)

    def finish(slot):
        bg_ref, bv_ref = slots[slot]
        cwg = cwg_ref[...]
        cwv = cwv_ref[...]
        cbg = cbg_ref[...]
        cbv = cbv_ref[...]
        for r in range(0, tm, FF_CHUNK):
            gate = conv(bg_ref, cwg, cbg, r)
            val = conv(bv_ref, cwv, cbv, r)
            act_ref[r:r + FF_CHUNK, :] = (_gelu_tanh(gate) * val).astype(BF16)

    @pl.when(j == 0)
    def _():
        @pl.when(i == 0)
        def _():
            cg_ref[...] = jnp.zeros(cg_ref.shape, F32)
            cv_ref[...] = jnp.zeros(cv_ref.shape, F32)

        xn_ref[...] = _rms(h_ref[...], gpre_ref[...]).astype(BF16)
        project(0)

    for parity in (0, 1):
        @pl.when((j > 0) & (j < nj) & (j % 2 == parity))
        def _():
            finish(1 - parity)
            project(parity)

    @pl.when(j == nj)
    def _():
        finish((nj - 1) % 2)


def _ffn_up(h2d, gpre, wg, wv, cwg, cwv, cbg, cbv, tm, tf, tiles_per_batch):
    N, D = h2d.shape
    F = wg.shape[1]
    nj = F // tf
    cur = lambda i, j: (0, jnp.minimum(j, nj - 1))
    prev = lambda i, j: (0, jnp.maximum(j - 1, 0))
    buf = lambda: pltpu.VMEM((CONV_PAD + tm, tf), F32)
    return pl.pallas_call(
        functools.partial(_ffn_up_body, tm=tm, nj=nj, tiles_per_batch=tiles_per_batch),
        grid=(N // tm, nj + 1),
        in_specs=[
            pl.BlockSpec((tm, D), lambda i, j: (i, 0)),
            _const_spec(gpre.shape),
            pl.BlockSpec((D, tf), cur),
            pl.BlockSpec((D, tf), cur),
            pl.BlockSpec((CONV_WIDTH, tf), prev), pl.BlockSpec((CONV_WIDTH, tf), prev),
            pl.BlockSpec((1, tf), prev), pl.BlockSpec((1, tf), prev),
        ],
        out_specs=pl.BlockSpec((tm, tf), lambda i, j: (i, jnp.maximum(j - 1, 0))),
        out_shape=jax.ShapeDtypeStruct((N, F), BF16),
        scratch_shapes=[
            pltpu.VMEM((tm, D), BF16),
            pltpu.VMEM((nj, CONV_PAD, tf), F32),
            pltpu.VMEM((nj, CONV_PAD, tf), F32),
            buf(), buf(), buf(), buf(),
        ],
        compiler_params=_params("arbitrary", "arbitrary"),
        name="ffn_up",
    )(h2d, gpre, wg, wv, cwg, cwv, cbg, cbv)


def _ffn_down_body(act_ref, h_ref, wd_ref, gpost_ref, o_ref):
    y = jnp.dot(act_ref[...], wd_ref[...], preferred_element_type=F32)
    o_ref[...] = h_ref[...] + _rms(y, gpost_ref[...])


def _ffn_down(act, h2d, wd, gpost, tm):
    N, D = h2d.shape
    F = act.shape[1]
    return pl.pallas_call(
        _ffn_down_body,
        grid=(N // tm,),
        in_specs=[
            pl.BlockSpec((tm, F), lambda i: (i, 0)),
            pl.BlockSpec((tm, D), lambda i: (i, 0)),
            _const_spec(wd.shape), _const_spec(gpost.shape),
        ],
        out_specs=pl.BlockSpec((tm, D), lambda i: (i, 0)),
        out_shape=jax.ShapeDtypeStruct((N, D), F32),
        compiler_params=_params("parallel"),
        name="ffn_down",
    )(act, h2d, wd, gpost)


def kernel(x, mem, w_in, q_norm, w_uq, kv_norm, w_ukv, mla_out_norm, hgrn_lb, hgrn_out_norm, w_out,
           ln_mix_pre, ln_mix_post, ln_x_pre, ln_x_post, mem_norm, w_xq, w_xk, w_xv, w_xo,
           ln_ffn_pre, ln_ffn_post, w_up, conv_w, conv_b, w_down):
    B, S, D = x.shape
    M = mem.shape[1]
    depth = w_in.shape[0]
    N = B * S
    tm = min(ROW_TILE, S)
    blk = min(ATT_BLK, S)
    bq = min(ATT_QBLK, S)
    lc = min(HG_BLK, S)
    assert S % tm == 0 and S % blk == 0 and S % bq == 0 and bq % blk == 0 and blk % CHUNK == 0
    assert S % lc == 0 and lc % HG_SUB == 0
    tiles_per_batch = S // tm
    d_ff = w_down.shape[1]
    tf = FF_TILE
    tmf = min(FF_ROWS, S)
    assert d_ff % tf == 0 and S % tmf == 0 and tmf % FF_CHUNK == 0
    n_mla_in = MLA_Q_RANK + MLA_KV_RANK + MLA_ROPE

    pos = jnp.arange(S, dtype=F32)
    inv_freq = 1.0 / (ROPE_THETA ** (jnp.arange(0, MLA_ROPE, 2, dtype=F32) / MLA_ROPE))
    ang = pos[:, None] * inv_freq[None, :]
    cos, sin = jnp.cos(ang), jnp.sin(ang)
    cosT, sinT = cos.T, sin.T

    row = lambda v: v.reshape(1, -1).astype(F32)
    h = x
    for l in range(depth):
        w_mla = w_in[l][:, :n_mla_in].astype(BF16)
        w_hg = w_in[l][:, n_mla_in:].astype(BF16)
        w_uqT = w_uq[l].T.astype(BF16)
        w_ukv_h = w_ukv[l].reshape(MLA_KV_RANK, MLA_HEADS, MLA_NOPE + MLA_V)
        w_uk = w_ukv_h[:, :, :MLA_NOPE].reshape(MLA_KV_RANK, MLA_HEADS * MLA_NOPE).astype(BF16)
        w_uvT = w_ukv_h[:, :, MLA_NOPE:].reshape(MLA_KV_RANK, MLA_HEADS * MLA_V).T.astype(BF16)
        w_out_a = w_out[l][:MLA_WIDTH].astype(BF16)
        w_out_r = w_out[l][MLA_WIDTH:].astype(BF16)
        w_xkv = jnp.concatenate([w_xk[l], w_xv[l]], axis=1).astype(BF16)
        w_up_g = w_up[l][:, :d_ff].astype(BF16)
        w_up_v = w_up[l][:, d_ff:].astype(BF16)

        qT, k, vT = _mla_prep(h, row(ln_mix_pre[l]), w_mla, row(q_norm[l]), w_uqT, row(kv_norm[l]),
                              w_uk, w_uvT, cos, sin, cosT, sinT, blk)
        a = _attention(qT, k, vT, bq, blk)
        zh = _norm_matmul(h.reshape(N, D), row(ln_mix_pre[l]), w_hg, F32, tm, w_hg.shape[1], "hgrn_in")
        r = _hgrn(zh.reshape(B, S, -1), hgrn_lb.astype(F32), row(hgrn_out_norm[l]), l, lc)
        h2d = _mix_out(a.reshape(N, -1), r.reshape(N, -1), h.reshape(N, D), row(mla_out_norm[l]),
                       w_out_a, w_out_r, row(ln_mix_post[l]), tm)

        kvm = _norm_matmul(mem.reshape(B * M, D), row(mem_norm[l]), w_xkv, BF16, min(256, B * M), 1024, "mem_kv")
        h2d = _xattn(h2d, row(ln_x_pre[l]), w_xq[l].astype(BF16), kvm.reshape(B, M, 2 * D),
                     w_xo[l].astype(BF16), row(ln_x_post[l]), tm, tiles_per_batch)

        act = _ffn_up(h2d, row(ln_ffn_pre[l]), w_up_g, w_up_v, conv_w[l][:, :d_ff], conv_w[l][:, d_ff:],
                      row(conv_b[l][:d_ff]), row(conv_b[l][d_ff:]), tmf, tf, S // tmf)
        h2d = _ffn_down(act, h2d, w_down[l].astype(BF16), row(ln_ffn_post[l]), min(FF_DOWN_ROWS, S))
        h = h2d.reshape(B, S, D)
    return h
```

```python
import functools
import math

import jax
import jax.numpy as jnp
from jax import lax
from jax.experimental import pallas as pl
from jax.experimental.pallas import tpu as pltpu

F32 = jnp.float32
BF16 = jnp.bfloat16

EPS = 1e-6
CHUNK = 64
MLA_V = 128
MLA_NOPE = 128
MLA_ROPE = 64
MLA_HEADS = 8
MLA_QK = MLA_NOPE + MLA_ROPE
MLA_KPAD = 256
MLA_Q_RANK = 512
MLA_KV_RANK = 256
ROPE_THETA = 10000.0
HG_DK = 128
HG_DV = 128
HG_HEADS = 8
MLA_WIDTH = MLA_HEADS * MLA_V
X_HEADS = 4
CONV_WIDTH = 3

LOG2E = 1.4426950408889634
NEG = -1e30

VMEM_LIMIT_BYTES = 56 * 1024 * 1024

ROW_TILE = 512
ATT_BLK = 1024
ATT_QBLK = 2048
ATT_MIN_MASS = 2.0 ** -80
HG_BLK = 256
HG_SUB = 16
SUBLANES = 8
FF_TILE = 512
FF_ROWS = 1024
FF_CHUNK = 64
FF_DOWN_ROWS = 256
MEM_ROWS = 256
MEM_COLS = 1024
CONV_PAD = 8


def _params(*sem):
    return pltpu.CompilerParams(dimension_semantics=sem, vmem_limit_bytes=VMEM_LIMIT_BYTES)


def _rms(xf, g):
    return xf * lax.rsqrt(jnp.mean(xf * xf, axis=-1, keepdims=True) + EPS) * g


def _const_spec(shape):
    nd = len(shape)
    return pl.BlockSpec(shape, lambda *_: (0,) * nd, pipeline_mode=pl.Buffered(1))


def _mla_prep_body(x_ref, g_ref, wm_ref, qn_ref, wuqT_ref, kvn_ref, wuk_ref, wuvT_ref,
                   cos_ref, sin_ref, cosT_ref, sinT_ref, qT_ref, k_ref, vT_ref, *, q_scale):
    xn = _rms(x_ref[0], g_ref[...]).astype(BF16)
    z = jnp.dot(xn, wm_ref[...], preferred_element_type=F32)
    cq = _rms(z[:, :MLA_Q_RANK], qn_ref[...]).astype(BF16)
    ckv = _rms(z[:, MLA_Q_RANK:MLA_Q_RANK + MLA_KV_RANK], kvn_ref[...]).astype(BF16)
    kr = z[:, MLA_Q_RANK + MLA_KV_RANK:]

    nt = (((1,), (1,)), ((), ()))
    qT = lax.dot_general(wuqT_ref[...], cq, nt, preferred_element_type=F32)
    cosT = cosT_ref[...]
    sinT = sinT_ref[...]
    half = MLA_ROPE // 2
    for h in range(MLA_HEADS):
        base = h * MLA_QK
        x1 = qT[base + MLA_NOPE:base + MLA_NOPE + half]
        x2 = qT[base + MLA_NOPE + half:base + MLA_QK]
        qT_ref[0, base:base + MLA_NOPE, :] = (qT[base:base + MLA_NOPE] * q_scale).astype(BF16)
        qT_ref[0, base + MLA_NOPE:base + MLA_NOPE + half, :] = ((x1 * cosT - x2 * sinT) * q_scale).astype(BF16)
        qT_ref[0, base + MLA_NOPE + half:base + MLA_QK, :] = ((x2 * cosT + x1 * sinT) * q_scale).astype(BF16)

    kn = jnp.dot(ckv, wuk_ref[...], preferred_element_type=F32)
    cos = cos_ref[...]
    sin = sin_ref[...]
    k1 = kr[:, :half]
    k2 = kr[:, half:]
    krope = jnp.concatenate([k1 * cos - k2 * sin, k2 * cos + k1 * sin], axis=-1).astype(BF16)
    lane = lax.broadcasted_iota(jnp.int32, (kr.shape[0], MLA_KPAD - MLA_QK), 1)
    one_hot = jnp.where(lane == 0, 1.0, 0.0).astype(BF16)
    for h in range(MLA_HEADS):
        k_ref[0, h, :, 0:MLA_NOPE] = kn[:, h * MLA_NOPE:(h + 1) * MLA_NOPE].astype(BF16)
        k_ref[0, h, :, MLA_NOPE:MLA_QK] = krope
        k_ref[0, h, :, MLA_QK:MLA_KPAD] = one_hot

    vT = lax.dot_general(wuvT_ref[...], ckv, nt, preferred_element_type=F32)
    for h in range(MLA_HEADS):
        vT_ref[0, h, 0] = vT[h * MLA_V:(h + 1) * MLA_V].astype(BF16)


def _mla_prep(x, g, wm, qn, wuqT, kvn, wuk, wuvT, cos, sin, cosT, sinT, blk):
    B, S, D = x.shape
    nb = S // blk
    half = MLA_ROPE // 2
    q_scale = (MLA_QK ** -0.5) * LOG2E
    return pl.pallas_call(
        functools.partial(_mla_prep_body, q_scale=q_scale),
        grid=(B, nb),
        in_specs=[
            pl.BlockSpec((1, blk, D), lambda b, i: (b, i, 0)),
            _const_spec(g.shape), _const_spec(wm.shape), _const_spec(qn.shape), _const_spec(wuqT.shape),
            _const_spec(kvn.shape), _const_spec(wuk.shape), _const_spec(wuvT.shape),
            pl.BlockSpec((blk, half), lambda b, i: (i, 0)),
            pl.BlockSpec((blk, half), lambda b, i: (i, 0)),
            pl.BlockSpec((half, blk), lambda b, i: (0, i)),
            pl.BlockSpec((half, blk), lambda b, i: (0, i)),
        ],
        out_specs=[
            pl.BlockSpec((1, MLA_HEADS * MLA_QK, blk), lambda b, i: (b, 0, i)),
            pl.BlockSpec((1, MLA_HEADS, blk, MLA_KPAD), lambda b, i: (b, 0, i, 0)),
            pl.BlockSpec((1, MLA_HEADS, 1, MLA_V, blk), lambda b, i: (b, 0, i, 0, 0)),
        ],
        out_shape=[
            jax.ShapeDtypeStruct((B, MLA_HEADS * MLA_QK, S), BF16),
            jax.ShapeDtypeStruct((B, MLA_HEADS, S, MLA_KPAD), BF16),
            jax.ShapeDtypeStruct((B, MLA_HEADS, nb, MLA_V, blk), BF16),
        ],
        compiler_params=_params("parallel", "parallel"),
        name="mla_prep",
    )(x, g, wm, qn, wuqT, kvn, wuk, wuvT, cos, sin, cosT, sinT)


def _attn_body(qT_ref, k_ref, vT_ref, o_ref, qa_ref, kn_ref, m_ref, l_ref, acc_ref, *, bq, bk):
    i = pl.program_id(2)
    ratio = bq // bk
    qT = qT_ref[0]
    qa_ref[0:MLA_QK, :] = qT

    def set_offset_row(offset):
        row = lax.broadcasted_iota(jnp.int32, (MLA_KPAD - MLA_QK, bq), 0)
        qa_ref[MLA_QK:, :] = jnp.where(row == 0, offset, 0.0).astype(BF16)

    @pl.when(i == 0)
    def _():
        def kmax(c, best):
            kc = k_ref[0, 0, pl.ds(pl.multiple_of(c * bk, bk), bk), :].astype(F32)
            sq = jnp.sum(kc * kc, axis=-1, keepdims=True)
            return jnp.maximum(best, jnp.max(sq, axis=0, keepdims=True))

        best = lax.fori_loop(0, k_ref.shape[2] // bk, kmax, jnp.zeros((1, 1), F32))
        kn_ref[...] = jnp.broadcast_to(jnp.sqrt(best), kn_ref.shape)

    def scores(j, c0, masked):
        kblk = k_ref[0, 0, pl.ds(pl.multiple_of(j * bk, bk), bk), :]
        s = jnp.dot(kblk, qa_ref[:, c0:], preferred_element_type=F32)
        if masked:
            key_chunk = lax.broadcasted_iota(jnp.int32, s.shape, 0) // CHUNK
            qry_chunk = lax.broadcasted_iota(jnp.int32, s.shape, 1) // CHUNK
            s = jnp.where(key_chunk <= qry_chunk, s, NEG)
        return s

    def sweep(update):
        l_ref[...] = jnp.zeros(l_ref.shape, F32)
        acc_ref[...] = jnp.zeros(acc_ref.shape, F32)

        def body(jj, carry):
            for u in range(ratio):
                update(jj * ratio + u, 0, False)
            return carry

        lax.fori_loop(0, i, body, 0)
        for d in range(ratio):
            update(i * ratio + d, d * bk, True)

    def fixed_update(j, c0, masked):
        p = jnp.exp2(scores(j, c0, masked))
        l_ref[:, c0:] += jnp.sum(p, axis=0, keepdims=True)
        acc_ref[:, c0:] += jnp.dot(vT_ref[0, 0, j], p.astype(BF16), preferred_element_type=F32)

    qf = qT.astype(F32)
    set_offset_row(-(jnp.sqrt(jnp.sum(qf * qf, axis=0, keepdims=True)) * kn_ref[...] + 1.0))
    sweep(fixed_update)

    def online_update(j, c0, masked):
        s = scores(j, c0, masked)
        m_prev = m_ref[:, c0:]
        m_new = jnp.maximum(m_prev, jnp.max(s, axis=0, keepdims=True))
        alpha = jnp.exp2(m_prev - m_new)
        p = jnp.exp2(s - m_new)
        l_ref[:, c0:] = alpha * l_ref[:, c0:] + jnp.sum(p, axis=0, keepdims=True)
        acc_ref[:, c0:] = alpha * acc_ref[:, c0:] + jnp.dot(vT_ref[0, 0, j], p.astype(BF16),
                                                            preferred_element_type=F32)
        m_ref[:, c0:] = m_new

    @pl.when(jnp.logical_not(jnp.min(l_ref[...]) >= ATT_MIN_MASS))
    def _():
        set_offset_row(jnp.zeros((1, bq), F32))
        m_ref[...] = jnp.full(m_ref.shape, NEG, F32)
        sweep(online_update)

    o_ref[0] = (acc_ref[...] / l_ref[...]).T


def _attention(qT, k, vT, bq, bk):
    B, H, S, _ = k.shape
    return pl.pallas_call(
        functools.partial(_attn_body, bq=bq, bk=bk),
        grid=(B, H, S // bq),
        in_specs=[
            pl.BlockSpec((1, MLA_QK, bq), lambda b, h, i: (b, h, i)),
            pl.BlockSpec((1, 1, S, MLA_KPAD), lambda b, h, i: (b, h, 0, 0)),
            pl.BlockSpec((1, 1, S // bk, MLA_V, bk), lambda b, h, i: (b, h, 0, 0, 0)),
        ],
        out_specs=pl.BlockSpec((1, bq, MLA_V), lambda b, h, i: (b, i, h)),
        out_shape=jax.ShapeDtypeStruct((B, S, H * MLA_V), F32),
        scratch_shapes=[pltpu.VMEM((MLA_KPAD, bq), BF16),
                        pltpu.VMEM((1, bq), F32), pltpu.VMEM((1, bq), F32), pltpu.VMEM((1, bq), F32),
                        pltpu.VMEM((MLA_V, bq), F32)],
        compiler_params=_params("parallel", "parallel", "arbitrary"),
        name="mla_attention",
    )(qT, k, vT)


def _norm_matmul_body(x_ref, g_ref, w_ref, o_ref, xn_ref):
    @pl.when(pl.program_id(1) == 0)
    def _():
        xn_ref[...] = _rms(x_ref[...], g_ref[...]).astype(BF16)

    o_ref[...] = jnp.dot(xn_ref[...], w_ref[...], preferred_element_type=F32).astype(o_ref.dtype)


def _norm_matmul(x2d, g, w, out_dtype, tm, tn, name):
    N, D = x2d.shape
    M = w.shape[1]
    w_spec = _const_spec(w.shape) if tn == M else pl.BlockSpec((D, tn), lambda i, j: (0, j))
    return pl.pallas_call(
        _norm_matmul_body,
        grid=(N // tm, M // tn),
        in_specs=[
            pl.BlockSpec((tm, D), lambda i, j: (i, 0)),
            _const_spec(g.shape),
            w_spec,
        ],
        out_specs=pl.BlockSpec((tm, tn), lambda i, j: (i, j)),
        out_shape=jax.ShapeDtypeStruct((N, M), out_dtype),
        scratch_shapes=[pltpu.VMEM((tm, D), BF16)],
        compiler_params=_params("parallel", "arbitrary"),
        name=name,
    )(x2d, g, w)


def _group_scan(x, row_in_group, group, reverse):
    n = x.shape[0]
    sh = 1
    while sh < group:
        if reverse:
            x = x + jnp.where(row_in_group < group - sh, pltpu.roll(x, n - sh, axis=0), 0.0)
        else:
            x = x + jnp.where(row_in_group >= sh, pltpu.roll(x, sh, axis=0), 0.0)
        sh *= 2
    return x


def _hgrn_body(hq_ref, hf_ref, hi_ref, hg_ref, lbp_ref, gain_ref, o_ref,
               st_ref, q_s, k_s, b_s, v_s, qd_s, kd_s, es_s, o_s, *, lc, layer):
    T = HG_SUB
    H = HG_HEADS

    @pl.when(pl.program_id(1) == 0)
    def _():
        st_ref[...] = jnp.zeros(st_ref.shape, F32)

    lbp = lbp_ref[...]
    e = jnp.exp(lbp - jnp.max(lbp, axis=0, keepdims=True))
    prob = e / jnp.sum(e, axis=0, keepdims=True)
    lb = jnp.sum(prob[:layer + 1], axis=0, keepdims=True)

    hq = hq_ref[0]
    q = hq * jax.nn.sigmoid(hq)
    f = lb + (1.0 - lb) * jax.nn.sigmoid(hf_ref[0])
    k = 1.0 - f
    logf = jnp.log(f)

    rig = lax.broadcasted_iota(jnp.int32, logf.shape, 0) % T
    b = _group_scan(logf, rig, T, reverse=False)
    suffix = _group_scan(logf, rig, T, reverse=True)
    staged = ((q_s, q), (k_s, k), (b_s, b * LOG2E), (v_s, hi_ref[0]), (qd_s, q * jnp.exp(b)),
              (kd_s, k * jnp.exp(suffix - logf)),
              (es_s, jnp.exp(suffix)))
    for ref, val in staged:
        for h in range(H):
            ref[h] = val[:, h * HG_DK:(h + 1) * HG_DK]

    R = SUBLANES
    groups = T // R
    row = lax.broadcasted_iota(jnp.int32, (R, HG_DK), 0)
    causal_bias = [jnp.where(row >= s, 0.0, NEG) for s in range(R)]
    nt = (((1,), (1,)), ((), ()))
    tn = (((0,), (0,)), ((), ()))

    def sub(i):
        r0 = i * T
        rg = [i * T + g * R for g in range(groups)]
        for h in range(H):
            q_g = [q_s[h, pl.ds(rg[g], R), :] for g in range(groups)]
            b_g = [b_s[h, pl.ds(rg[g], R), :] for g in range(groups)]
            acc = [jnp.zeros((R, HG_DV), F32) for _ in range(groups)]
            for s in range(T):
                ks = k_s[h, pl.ds(r0 + s, 1), :]
                bs = b_s[h, pl.ds(r0 + s, 1), :]
                vs = v_s[h, pl.ds(r0 + s, 1), :]
                for g in range(s // R, groups):
                    diff = b_g[g] - bs
                    if g == s // R:
                        diff = diff + causal_bias[s % R]
                    col = jnp.sum(q_g[g] * ks * jnp.exp2(diff), axis=-1, keepdims=True)
                    acc[g] = acc[g] + col * vs
            st = st_ref[h]
            inter = lax.dot_general(qd_s[h, pl.ds(r0, T), :], st, nt, preferred_element_type=F32)
            for g in range(groups):
                o_s[h, pl.ds(rg[g], R), :] = acc[g] + inter[g * R:(g + 1) * R]
            upd = lax.dot_general(v_s[h, pl.ds(r0, T), :], kd_s[h, pl.ds(r0, T), :], tn,
                                  preferred_element_type=F32)
            st_ref[h] = st * es_s[h, pl.ds(r0, 1), :] + upd

    for i in range(lc // T):
        sub(i)

    gain = gain_ref[...]
    for h in range(H):
        cs = slice(h * HG_DV, (h + 1) * HG_DV)
        hg = hg_ref[0, :, cs]
        o_ref[0, :, cs] = _rms(o_s[h], gain) * (hg * jax.nn.sigmoid(hg))


def _hgrn(zh, lb_table, gain, layer, lc):
    B, S, _ = zh.shape
    W = HG_HEADS * HG_DK
    blk = lambda part: pl.BlockSpec((1, lc, W), lambda b, c: (b, c, part))
    scr = lambda: pltpu.VMEM((HG_HEADS, lc, HG_DK), F32)
    return pl.pallas_call(
        functools.partial(_hgrn_body, lc=lc, layer=layer),
        grid=(B, S // lc),
        in_specs=[
            blk(0), blk(1), blk(2), blk(3),
            pl.BlockSpec(lb_table.shape, lambda b, c: (0, 0)),
            pl.BlockSpec((1, HG_DV), lambda b, c: (0, 0)),
        ],
        out_specs=pl.BlockSpec((1, lc, W), lambda b, c: (b, c, 0)),
        out_shape=jax.ShapeDtypeStruct((B, S, W), F32),
        scratch_shapes=[pltpu.VMEM((HG_HEADS, HG_DV, HG_DK), F32)] + [scr() for _ in range(8)],
        compiler_params=_params("parallel", "arbitrary"),
        name="hgrn2",
    )(zh, zh, zh, zh, lb_table, gain)


def _mix_out_body(a_ref, r_ref, x_ref, gm_ref, wa_ref, wr_ref, gp_ref, o_ref):
    an = _rms(a_ref[...], gm_ref[...]).astype(BF16)
    y = jnp.dot(an, wa_ref[...], preferred_element_type=F32)
    y = y + jnp.dot(r_ref[...].astype(BF16), wr_ref[...], preferred_element_type=F32)
    o_ref[...] = x_ref[...] + _rms(y, gp_ref[...])


def _mix_out(a2d, r2d, x2d, gm, wa, wr, gp, tm):
    N, D = x2d.shape
    return pl.pallas_call(
        _mix_out_body,
        grid=(N // tm,),
        in_specs=[
            pl.BlockSpec((tm, a2d.shape[1]), lambda i: (i, 0)),
            pl.BlockSpec((tm, r2d.shape[1]), lambda i: (i, 0)),
            pl.BlockSpec((tm, D), lambda i: (i, 0)),
            _const_spec(gm.shape), _const_spec(wa.shape), _const_spec(wr.shape), _const_spec(gp.shape),
        ],
        out_specs=pl.BlockSpec((tm, D), lambda i: (i, 0)),
        out_shape=jax.ShapeDtypeStruct((N, D), F32),
        compiler_params=_params("parallel"),
        name="mix_out",
    )(a2d, r2d, x2d, gm, wa, wr, gp)


def _xattn_body(h_ref, gpre_ref, wq_ref, kv_ref, wo_ref, gpost_ref, o_ref, oh_ref, *, q_scale):
    h = h_ref[...]
    D = h.shape[-1]
    dh = D // X_HEADS
    xn = _rms(h, gpre_ref[...]).astype(BF16)
    qx = (jnp.dot(xn, wq_ref[...], preferred_element_type=F32) * q_scale).astype(BF16)
    for hd in range(X_HEADS):
        qh = qx[:, hd * dh:(hd + 1) * dh]
        kh = kv_ref[0, :, hd * dh:(hd + 1) * dh]
        vh = kv_ref[0, :, D + hd * dh:D + (hd + 1) * dh]
        s = lax.dot_general(qh, kh, (((1,), (1,)), ((), ())), preferred_element_type=F32)
        p = jnp.exp2(s - jnp.max(s, axis=-1, keepdims=True))
        l = jnp.sum(p, axis=-1, keepdims=True)
        oh = jnp.dot(p.astype(BF16), vh, preferred_element_type=F32) / l
        oh_ref[:, hd * dh:(hd + 1) * dh] = oh.astype(BF16)
    ox = jnp.dot(oh_ref[...], wo_ref[...], preferred_element_type=F32)
    o_ref[...] = h + _rms(ox, gpost_ref[...])


def _xattn(h2d, gpre, wq, kvm, wo, gpost, tm, tiles_per_batch):
    N, D = h2d.shape
    M = kvm.shape[1]
    q_scale = ((D // X_HEADS) ** -0.5) * LOG2E
    return pl.pallas_call(
        functools.partial(_xattn_body, q_scale=q_scale),
        grid=(N // tm,),
        in_specs=[
            pl.BlockSpec((tm, D), lambda i: (i, 0)),
            _const_spec(gpre.shape), _const_spec(wq.shape),
            pl.BlockSpec((1, M, 2 * D), lambda i: (i // tiles_per_batch, 0, 0)),
            _const_spec(wo.shape), _const_spec(gpost.shape),
        ],
        out_specs=pl.BlockSpec((tm, D), lambda i: (i, 0)),
        out_shape=jax.ShapeDtypeStruct((N, D), F32),
        scratch_shapes=[pltpu.VMEM((tm, D), BF16)],
        compiler_params=_params("parallel"),
        name="xattn",
    )(h2d, gpre, wq, kvm, wo, gpost)


def _gelu_tanh(x):
    return x * (0.5 * (1.0 + jnp.tanh(math.sqrt(2.0 / math.pi) * (x + 0.044715 * (x * x * x)))))


def _ffn_up_body(h_ref, gpre_ref, wg_ref, wv_ref, cwg_ref, cwv_ref, cbg_ref, cbv_ref,
                 act_ref, xn_ref, cg_ref, cv_ref, bg0_ref, bv0_ref, bg1_ref, bv1_ref,
                 *, tm, nj, tiles_per_batch):
    i = pl.program_id(0)
    j = pl.program_id(1)
    first = (i % tiles_per_batch) == 0
    slots = ((bg0_ref, bv0_ref), (bg1_ref, bv1_ref))

    def project(slot):
        for w_ref, carry_ref, buf_ref in zip((wg_ref, wv_ref), (cg_ref, cv_ref), slots[slot]):
            u = jnp.dot(xn_ref[...], w_ref[...], preferred_element_type=F32)
            buf_ref[0:CONV_PAD, :] = jnp.where(first, 0.0, carry_ref[j])
            buf_ref[CONV_PAD:CONV_PAD + tm, :] = u
            carry_ref[j] = u[tm - CONV_PAD:, :]

    def conv(buf_ref, cw, cb, r):
        base = CONV_PAD + r
        return (cw[0:1, :] * buf_ref[base - 2:base - 2 + FF_CHUNK, :]
                + cw[1:2, :] * buf_ref[base - 1:base - 1 + FF_CHUNK, :]
                + cw[2:3, :] * buf_ref[base:base + FF_CHUNK, :] + cb)

    def finish(slot):
        bg_ref, bv_ref = slots[slot]
        cwg = cwg_ref[...]
        cwv = cwv_ref[...]
        cbg = cbg_ref[...]
        cbv = cbv_ref[...]
        for r in range(0, tm, FF_CHUNK):
            gate = conv(bg_ref, cwg, cbg, r)
            val = conv(bv_ref, cwv, cbv, r)
            act_ref[r:r + FF_CHUNK, :] = (_gelu_tanh(gate) * val).astype(BF16)

    @pl.when(j == 0)
    def _():
        @pl.when(i == 0)
        def _():
            cg_ref[...] = jnp.zeros(cg_ref.shape, F32)
            cv_ref[...] = jnp.zeros(cv_ref.shape, F32)

        xn_ref[...] = _rms(h_ref[...], gpre_ref[...]).astype(BF16)
        project(0)

    for parity in (0, 1):
        @pl.when((j > 0) & (j < nj) & (j % 2 == parity))
        def _():
            finish(1 - parity)
            project(parity)

    @pl.when(j == nj)
    def _():
        finish((nj - 1) % 2)


def _ffn_up(h2d, gpre, wg, wv, cwg, cwv, cbg, cbv, tm, tf, tiles_per_batch):
    N, D = h2d.shape
    F = wg.shape[1]
    nj = F // tf
    cur = lambda i, j: (0, jnp.minimum(j, nj - 1))
    prev = lambda i, j: (0, jnp.maximum(j - 1, 0))
    buf = lambda: pltpu.VMEM((CONV_PAD + tm, tf), F32)
    return pl.pallas_call(
        functools.partial(_ffn_up_body, tm=tm, nj=nj, tiles_per_batch=tiles_per_batch),
        grid=(N // tm, nj + 1),
        in_specs=[
            pl.BlockSpec((tm, D), lambda i, j: (i, 0)),
            _const_spec(gpre.shape),
            pl.BlockSpec((D, tf), cur),
            pl.BlockSpec((D, tf), cur),
            pl.BlockSpec((CONV_WIDTH, tf), prev), pl.BlockSpec((CONV_WIDTH, tf), prev),
            pl.BlockSpec((1, tf), prev), pl.BlockSpec((1, tf), prev),
        ],
        out_specs=pl.BlockSpec((tm, tf), lambda i, j: (i, jnp.maximum(j - 1, 0))),
        out_shape=jax.ShapeDtypeStruct((N, F), BF16),
        scratch_shapes=[
            pltpu.VMEM((tm, D), BF16),
            pltpu.VMEM((nj, CONV_PAD, tf), F32),
            pltpu.VMEM((nj, CONV_PAD, tf), F32),
            buf(), buf(), buf(), buf(),
        ],
        compiler_params=_params("arbitrary", "arbitrary"),
        name="ffn_up",
    )(h2d, gpre, wg, wv, cwg, cwv, cbg, cbv)


def _ffn_down_body(act_ref, h_ref, wd_ref, gpost_ref, o_ref):
    y = jnp.dot(act_ref[...], wd_ref[...], preferred_element_type=F32)
    o_ref[...] = h_ref[...] + _rms(y, gpost_ref[...])


def _ffn_down(act, h2d, wd, gpost, tm):
    N, D = h2d.shape
    F = act.shape[1]
    return pl.pallas_call(
        _ffn_down_body,
        grid=(N // tm,),
        in_specs=[
            pl.BlockSpec((tm, F), lambda i: (i, 0)),
            pl.BlockSpec((tm, D), lambda i: (i, 0)),
            _const_spec(wd.shape), _const_spec(gpost.shape),
        ],
        out_specs=pl.BlockSpec((tm, D), lambda i: (i, 0)),
        out_shape=jax.ShapeDtypeStruct((N, D), F32),
        compiler_params=_params("parallel"),
        name="ffn_down",
    )(act, h2d, wd, gpost)


def kernel(x, mem, w_in, q_norm, w_uq, kv_norm, w_ukv, mla_out_norm, hgrn_lb, hgrn_out_norm, w_out,
           ln_mix_pre, ln_mix_post, ln_x_pre, ln_x_post, mem_norm, w_xq, w_xk, w_xv, w_xo,
           ln_ffn_pre, ln_ffn_post, w_up, conv_w, conv_b, w_down):
    B, S, D = x.shape
    M = mem.shape[1]
    depth = w_in.shape[0]
    N = B * S
    tm = min(ROW_TILE, S)
    blk = min(ATT_BLK, S)
    bq = min(ATT_QBLK, S)
    lc = min(HG_BLK, S)
    assert S % tm == 0 and S % blk == 0 and S % bq == 0 and bq % blk == 0 and blk % CHUNK == 0
    assert S % lc == 0 and lc % HG_SUB == 0
    tiles_per_batch = S // tm
    d_ff = w_down.shape[1]
    tf = FF_TILE
    tmf = min(FF_ROWS, S)
    assert d_ff % tf == 0 and S % tmf == 0 and tmf % FF_CHUNK == 0
    n_mla_in = MLA_Q_RANK + MLA_KV_RANK + MLA_ROPE

    pos = jnp.arange(S, dtype=F32)
    inv_freq = 1.0 / (ROPE_THETA ** (jnp.arange(0, MLA_ROPE, 2, dtype=F32) / MLA_ROPE))
    ang = pos[:, None] * inv_freq[None, :]
    cos, sin = jnp.cos(ang), jnp.sin(ang)
    cosT, sinT = cos.T, sin.T

    row = lambda v: v.reshape(1, -1).astype(F32)
    h = x
    for l in range(depth):
        w_mla = w_in[l][:, :n_mla_in].astype(BF16)
        w_hg = w_in[l][:, n_mla_in:].astype(BF16)
        w_uqT = w_uq[l].T.astype(BF16)
        w_ukv_h = w_ukv[l].reshape(MLA_KV_RANK, MLA_HEADS, MLA_NOPE + MLA_V)
        w_uk = w_ukv_h[:, :, :MLA_NOPE].reshape(MLA_KV_RANK, MLA_HEADS * MLA_NOPE).astype(BF16)
        w_uvT = w_ukv_h[:, :, MLA_NOPE:].reshape(MLA_KV_RANK, MLA_HEADS * MLA_V).T.astype(BF16)
        w_out_a = w_out[l][:MLA_WIDTH].astype(BF16)
        w_out_r = w_out[l][MLA_WIDTH:].astype(BF16)
        w_xkv = jnp.concatenate([w_xk[l], w_xv[l]], axis=1).astype(BF16)
        w_up_g = w_up[l][:, :d_ff].astype(BF16)
        w_up_v = w_up[l][:, d_ff:].astype(BF16)

        qT, k, vT = _mla_prep(h, row(ln_mix_pre[l]), w_mla, row(q_norm[l]), w_uqT, row(kv_norm[l]),
                              w_uk, w_uvT, cos, sin, cosT, sinT, blk)
        a = _attention(qT, k, vT, bq, blk)
        zh = _norm_matmul(h.reshape(N, D), row(ln_mix_pre[l]), w_hg, F32, tm, w_hg.shape[1], "hgrn_in")
        r = _hgrn(zh.reshape(B, S, -1), hgrn_lb.astype(F32), row(hgrn_out_norm[l]), l, lc)
        h2d = _mix_out(a.reshape(N, -1), r.reshape(N, -1), h.reshape(N, D), row(mla_out_norm[l]),
                       w_out_a, w_out_r, row(ln_mix_post[l]), tm)

        kvm = _norm_matmul(mem.reshape(B * M, D), row(mem_norm[l]), w_xkv, BF16, min(MEM_ROWS, B * M),
                           MEM_COLS, "mem_kv")
        h2d = _xattn(h2d, row(ln_x_pre[l]), w_xq[l].astype(BF16), kvm.reshape(B, M, 2 * D),
                     w_xo[l].astype(BF16), row(ln_x_post[l]), tm, tiles_per_batch)

        act = _ffn_up(h2d, row(ln_ffn_pre[l]), w_up_g, w_up_v, conv_w[l][:, :d_ff], conv_w[l][:, d_ff:],
                      row(conv_b[l][:d_ff]), row(conv_b[l][d_ff:]), tmf, tf, S // tmf)
        h2d = _ffn_down(act, h2d, w_down[l].astype(BF16), row(ln_ffn_post[l]), min(FF_DOWN_ROWS, S))
        h = h2d.reshape(B, S, D)
    return h
```

```python
import functools
import math

import jax
import jax.numpy as jnp
from jax import lax
from jax.experimental import pallas as pl
from jax.experimental.pallas import tpu as pltpu

F32 = jnp.float32
BF16 = jnp.bfloat16

EPS = 1e-6
CHUNK = 64
MLA_V = 128
MLA_NOPE = 128
MLA_ROPE = 64
MLA_HEADS = 8
MLA_QK = MLA_NOPE + MLA_ROPE
MLA_KPAD = 256
MLA_Q_RANK = 512
MLA_KV_RANK = 256
ROPE_THETA = 10000.0
HG_DK = 128
HG_DV = 128
HG_HEADS = 8
MLA_WIDTH = MLA_HEADS * MLA_V
X_HEADS = 4
CONV_WIDTH = 3

LOG2E = 1.4426950408889634
NEG = -1e30

VMEM_LIMIT_BYTES = 56 * 1024 * 1024

ROW_TILE = 512
ATT_BLK = 1024
ATT_QBLK = 2048
ATT_DIAG = 512
ATT_MIN_MASS = 2.0 ** -80
HG_BLK = 256
HG_SUB = 16
SUBLANES = 8
FF_TILE = 512
FF_ROWS = 1024
FF_CHUNK = 64
FF_DOWN_ROWS = 256
MEM_ROWS = 256
MEM_COLS = 1024
CONV_PAD = 8


def _params(*sem):
    return pltpu.CompilerParams(dimension_semantics=sem, vmem_limit_bytes=VMEM_LIMIT_BYTES)


def _rms(xf, g):
    return xf * lax.rsqrt(jnp.mean(xf * xf, axis=-1, keepdims=True) + EPS) * g


def _const_spec(shape):
    nd = len(shape)
    return pl.BlockSpec(shape, lambda *_: (0,) * nd, pipeline_mode=pl.Buffered(1))


def _mla_prep_body(x_ref, g_ref, wm_ref, qn_ref, wuqT_ref, kvn_ref, wuk_ref, wuvT_ref,
                   cos_ref, sin_ref, cosT_ref, sinT_ref, qT_ref, k_ref, vT_ref, *, q_scale):
    xn = _rms(x_ref[0], g_ref[...]).astype(BF16)
    z = jnp.dot(xn, wm_ref[...], preferred_element_type=F32)
    cq = _rms(z[:, :MLA_Q_RANK], qn_ref[...]).astype(BF16)
    ckv = _rms(z[:, MLA_Q_RANK:MLA_Q_RANK + MLA_KV_RANK], kvn_ref[...]).astype(BF16)
    kr = z[:, MLA_Q_RANK + MLA_KV_RANK:]

    nt = (((1,), (1,)), ((), ()))
    qT = lax.dot_general(wuqT_ref[...], cq, nt, preferred_element_type=F32)
    cosT = cosT_ref[...]
    sinT = sinT_ref[...]
    half = MLA_ROPE // 2
    for h in range(MLA_HEADS):
        base = h * MLA_QK
        x1 = qT[base + MLA_NOPE:base + MLA_NOPE + half]
        x2 = qT[base + MLA_NOPE + half:base + MLA_QK]
        qT_ref[0, base:base + MLA_NOPE, :] = (qT[base:base + MLA_NOPE] * q_scale).astype(BF16)
        qT_ref[0, base + MLA_NOPE:base + MLA_NOPE + half, :] = ((x1 * cosT - x2 * sinT) * q_scale).astype(BF16)
        qT_ref[0, base + MLA_NOPE + half:base + MLA_QK, :] = ((x2 * cosT + x1 * sinT) * q_scale).astype(BF16)

    kn = jnp.dot(ckv, wuk_ref[...], preferred_element_type=F32)
    cos = cos_ref[...]
    sin = sin_ref[...]
    k1 = kr[:, :half]
    k2 = kr[:, half:]
    krope = jnp.concatenate([k1 * cos - k2 * sin, k2 * cos + k1 * sin], axis=-1).astype(BF16)
    lane = lax.broadcasted_iota(jnp.int32, (kr.shape[0], MLA_KPAD - MLA_QK), 1)
    one_hot = jnp.where(lane == 0, 1.0, 0.0).astype(BF16)
    for h in range(MLA_HEADS):
        k_ref[0, h, :, 0:MLA_NOPE] = kn[:, h * MLA_NOPE:(h + 1) * MLA_NOPE].astype(BF16)
        k_ref[0, h, :, MLA_NOPE:MLA_QK] = krope
        k_ref[0, h, :, MLA_QK:MLA_KPAD] = one_hot

    vT = lax.dot_general(wuvT_ref[...], ckv, nt, preferred_element_type=F32)
    for h in range(MLA_HEADS):
        vT_ref[0, h, 0] = vT[h * MLA_V:(h + 1) * MLA_V].astype(BF16)


def _mla_prep(x, g, wm, qn, wuqT, kvn, wuk, wuvT, cos, sin, cosT, sinT, blk):
    B, S, D = x.shape
    nb = S // blk
    half = MLA_ROPE // 2
    q_scale = (MLA_QK ** -0.5) * LOG2E
    return pl.pallas_call(
        functools.partial(_mla_prep_body, q_scale=q_scale),
        grid=(B, nb),
        in_specs=[
            pl.BlockSpec((1, blk, D), lambda b, i: (b, i, 0)),
            _const_spec(g.shape), _const_spec(wm.shape), _const_spec(qn.shape), _const_spec(wuqT.shape),
            _const_spec(kvn.shape), _const_spec(wuk.shape), _const_spec(wuvT.shape),
            pl.BlockSpec((blk, half), lambda b, i: (i, 0)),
            pl.BlockSpec((blk, half), lambda b, i: (i, 0)),
            pl.BlockSpec((half, blk), lambda b, i: (0, i)),
            pl.BlockSpec((half, blk), lambda b, i: (0, i)),
        ],
        out_specs=[
            pl.BlockSpec((1, MLA_HEADS * MLA_QK, blk), lambda b, i: (b, 0, i)),
            pl.BlockSpec((1, MLA_HEADS, blk, MLA_KPAD), lambda b, i: (b, 0, i, 0)),
            pl.BlockSpec((1, MLA_HEADS, 1, MLA_V, blk), lambda b, i: (b, 0, i, 0, 0)),
        ],
        out_shape=[
            jax.ShapeDtypeStruct((B, MLA_HEADS * MLA_QK, S), BF16),
            jax.ShapeDtypeStruct((B, MLA_HEADS, S, MLA_KPAD), BF16),
            jax.ShapeDtypeStruct((B, MLA_HEADS, nb, MLA_V, blk), BF16),
        ],
        compiler_params=_params("parallel", "parallel"),
        name="mla_prep",
    )(x, g, wm, qn, wuqT, kvn, wuk, wuvT, cos, sin, cosT, sinT)


def _attn_body(qT_ref, k_ref, vT_ref, o_ref, qa_ref, kn_ref, m_ref, l_ref, acc_ref, *, bq, bk):
    i = pl.program_id(2)
    ratio = bq // bk
    qT = qT_ref[0]
    qa_ref[0:MLA_QK, :] = qT

    def set_offset_row(offset):
        row = lax.broadcasted_iota(jnp.int32, (MLA_KPAD - MLA_QK, bq), 0)
        qa_ref[MLA_QK:, :] = jnp.where(row == 0, offset, 0.0).astype(BF16)

    @pl.when(i == 0)
    def _():
        def kmax(c, best):
            kc = k_ref[0, 0, pl.ds(pl.multiple_of(c * bk, bk), bk), :].astype(F32)
            sq = jnp.sum(kc * kc, axis=-1, keepdims=True)
            return jnp.maximum(best, jnp.max(sq, axis=0, keepdims=True))

        best = lax.fori_loop(0, k_ref.shape[2] // bk, kmax, jnp.zeros((1, 1), F32))
        kn_ref[...] = jnp.broadcast_to(jnp.sqrt(best), kn_ref.shape)

    def keys_values(j, part):
        off, n = (0, bk) if part is None else part
        kblk = k_ref[0, 0, pl.ds(pl.multiple_of(j * bk + off, n), n), :]
        vblk = vT_ref[0, 0, j]
        return kblk, (vblk if part is None else vblk[:, off:off + n])

    def scores(kblk, c0, masked):
        s = jnp.dot(kblk, qa_ref[:, c0:], preferred_element_type=F32)
        if masked:
            key_chunk = lax.broadcasted_iota(jnp.int32, s.shape, 0) // CHUNK
            qry_chunk = lax.broadcasted_iota(jnp.int32, s.shape, 1) // CHUNK
            s = jnp.where(key_chunk <= qry_chunk, s, NEG)
        return s

    def sweep(update):
        l_ref[...] = jnp.zeros(l_ref.shape, F32)
        acc_ref[...] = jnp.zeros(acc_ref.shape, F32)

        def body(jj, carry):
            for u in range(ratio):
                update(jj * ratio + u, 0, False, None)
            return carry

        lax.fori_loop(0, i, body, 0)
        for c0 in range(0, bq, ATT_DIAG):
            update(i * ratio + c0 // bk, c0, True, (c0 % bk, ATT_DIAG))

    def fixed_update(j, c0, masked, part):
        kblk, vblk = keys_values(j, part)
        p = jnp.exp2(scores(kblk, c0, masked))
        l_ref[:, c0:] += jnp.sum(p, axis=0, keepdims=True)
        acc_ref[:, c0:] += jnp.dot(vblk, p.astype(BF16), preferred_element_type=F32)

    qf = qT.astype(F32)
    set_offset_row(-(jnp.sqrt(jnp.sum(qf * qf, axis=0, keepdims=True)) * kn_ref[...] + 1.0))
    sweep(fixed_update)

    def online_update(j, c0, masked, part):
        kblk, vblk = keys_values(j, part)
        s = scores(kblk, c0, masked)
        m_prev = m_ref[:, c0:]
        m_new = jnp.maximum(m_prev, jnp.max(s, axis=0, keepdims=True))
        alpha = jnp.exp2(m_prev - m_new)
        p = jnp.exp2(s - m_new)
        l_ref[:, c0:] = alpha * l_ref[:, c0:] + jnp.sum(p, axis=0, keepdims=True)
        acc_ref[:, c0:] = alpha * acc_ref[:, c0:] + jnp.dot(vblk, p.astype(BF16),
                                                            preferred_element_type=F32)
        m_ref[:, c0:] = m_new

    @pl.when(jnp.logical_not(jnp.min(l_ref[...]) >= ATT_MIN_MASS))
    def _():
        set_offset_row(jnp.zeros((1, bq), F32))
        m_ref[...] = jnp.full(m_ref.shape, NEG, F32)
        sweep(online_update)

    o_ref[0] = (acc_ref[...] / l_ref[...]).T


def _attention(qT, k, vT, bq, bk):
    B, H, S, _ = k.shape
    return pl.pallas_call(
        functools.partial(_attn_body, bq=bq, bk=bk),
        grid=(B, H, S // bq),
        in_specs=[
            pl.BlockSpec((1, MLA_QK, bq), lambda b, h, i: (b, h, i)),
            pl.BlockSpec((1, 1, S, MLA_KPAD), lambda b, h, i: (b, h, 0, 0)),
            pl.BlockSpec((1, 1, S // bk, MLA_V, bk), lambda b, h, i: (b, h, 0, 0, 0)),
        ],
        out_specs=pl.BlockSpec((1, bq, MLA_V), lambda b, h, i: (b, i, h)),
        out_shape=jax.ShapeDtypeStruct((B, S, H * MLA_V), F32),
        scratch_shapes=[pltpu.VMEM((MLA_KPAD, bq), BF16),
                        pltpu.VMEM((1, bq), F32), pltpu.VMEM((1, bq), F32), pltpu.VMEM((1, bq), F32),
                        pltpu.VMEM((MLA_V, bq), F32)],
        compiler_params=_params("parallel", "parallel", "arbitrary"),
        name="mla_attention",
    )(qT, k, vT)


def _norm_matmul_body(x_ref, g_ref, w_ref, o_ref, xn_ref):
    @pl.when(pl.program_id(1) == 0)
    def _():
        xn_ref[...] = _rms(x_ref[...], g_ref[...]).astype(BF16)

    o_ref[...] = jnp.dot(xn_ref[...], w_ref[...], preferred_element_type=F32).astype(o_ref.dtype)


def _norm_matmul(x2d, g, w, out_dtype, tm, tn, name):
    N, D = x2d.shape
    M = w.shape[1]
    w_spec = _const_spec(w.shape) if tn == M else pl.BlockSpec((D, tn), lambda i, j: (0, j))
    return pl.pallas_call(
        _norm_matmul_body,
        grid=(N // tm, M // tn),
        in_specs=[
            pl.BlockSpec((tm, D), lambda i, j: (i, 0)),
            _const_spec(g.shape),
            w_spec,
        ],
        out_specs=pl.BlockSpec((tm, tn), lambda i, j: (i, j)),
        out_shape=jax.ShapeDtypeStruct((N, M), out_dtype),
        scratch_shapes=[pltpu.VMEM((tm, D), BF16)],
        compiler_params=_params("parallel", "arbitrary"),
        name=name,
    )(x2d, g, w)


def _group_scan(x, row_in_group, group, reverse):
    n = x.shape[0]
    sh = 1
    while sh < group:
        if reverse:
            x = x + jnp.where(row_in_group < group - sh, pltpu.roll(x, n - sh, axis=0), 0.0)
        else:
            x = x + jnp.where(row_in_group >= sh, pltpu.roll(x, sh, axis=0), 0.0)
        sh *= 2
    return x


def _hgrn_body(hq_ref, hf_ref, hi_ref, hg_ref, lbp_ref, gain_ref, o_ref,
               st_ref, q_s, k_s, b_s, v_s, qd_s, kd_s, es_s, o_s, *, lc, layer):
    T = HG_SUB
    H = HG_HEADS

    @pl.when(pl.program_id(1) == 0)
    def _():
        st_ref[...] = jnp.zeros(st_ref.shape, F32)

    lbp = lbp_ref[...]
    e = jnp.exp(lbp - jnp.max(lbp, axis=0, keepdims=True))
    prob = e / jnp.sum(e, axis=0, keepdims=True)
    lb = jnp.sum(prob[:layer + 1], axis=0, keepdims=True)

    hq = hq_ref[0]
    q = hq * jax.nn.sigmoid(hq)
    f = lb + (1.0 - lb) * jax.nn.sigmoid(hf_ref[0])
    k = 1.0 - f
    logf = jnp.log(f)

    rig = lax.broadcasted_iota(jnp.int32, logf.shape, 0) % T
    b = _group_scan(logf, rig, T, reverse=False)
    suffix = _group_scan(logf, rig, T, reverse=True)
    staged = ((q_s, q), (k_s, k), (b_s, b * LOG2E), (v_s, hi_ref[0]), (qd_s, q * jnp.exp(b)),
              (kd_s, k * jnp.exp(suffix - logf)),
              (es_s, jnp.exp(suffix)))
    for ref, val in staged:
        for h in range(H):
            ref[h] = val[:, h * HG_DK:(h + 1) * HG_DK]

    R = SUBLANES
    groups = T // R
    row = lax.broadcasted_iota(jnp.int32, (R, HG_DK), 0)
    causal_bias = [jnp.where(row >= s, 0.0, NEG) for s in range(R)]
    nt = (((1,), (1,)), ((), ()))
    tn = (((0,), (0,)), ((), ()))

    def sub(i):
        r0 = i * T
        rg = [i * T + g * R for g in range(groups)]
        for h in range(H):
            q_g = [q_s[h, pl.ds(rg[g], R), :] for g in range(groups)]
            b_g = [b_s[h, pl.ds(rg[g], R), :] for g in range(groups)]
            acc = [jnp.zeros((R, HG_DV), F32) for _ in range(groups)]
            for s in range(T):
                ks = k_s[h, pl.ds(r0 + s, 1), :]
                bs = b_s[h, pl.ds(r0 + s, 1), :]
                vs = v_s[h, pl.ds(r0 + s, 1), :]
                for g in range(s // R, groups):
                    diff = b_g[g] - bs
                    if g == s // R:
                        diff = diff + causal_bias[s % R]
                    col = jnp.sum(q_g[g] * ks * jnp.exp2(diff), axis=-1, keepdims=True)
                    acc[g] = acc[g] + col * vs
            st = st_ref[h]
            inter = lax.dot_general(qd_s[h, pl.ds(r0, T), :], st, nt, preferred_element_type=F32)
            for g in range(groups):
                o_s[h, pl.ds(rg[g], R), :] = acc[g] + inter[g * R:(g + 1) * R]
            upd = lax.dot_general(v_s[h, pl.ds(r0, T), :], kd_s[h, pl.ds(r0, T), :], tn,
                                  preferred_element_type=F32)
            st_ref[h] = st * es_s[h, pl.ds(r0, 1), :] + upd

    for i in range(lc // T):
        sub(i)

    gain = gain_ref[...]
    for h in range(H):
        cs = slice(h * HG_DV, (h + 1) * HG_DV)
        hg = hg_ref[0, :, cs]
        o_ref[0, :, cs] = _rms(o_s[h], gain) * (hg * jax.nn.sigmoid(hg))


def _hgrn(zh, lb_table, gain, layer, lc):
    B, S, _ = zh.shape
    W = HG_HEADS * HG_DK
    blk = lambda part: pl.BlockSpec((1, lc, W), lambda b, c: (b, c, part))
    scr = lambda: pltpu.VMEM((HG_HEADS, lc, HG_DK), F32)
    return pl.pallas_call(
        functools.partial(_hgrn_body, lc=lc, layer=layer),
        grid=(B, S // lc),
        in_specs=[
            blk(0), blk(1), blk(2), blk(3),
            pl.BlockSpec(lb_table.shape, lambda b, c: (0, 0)),
            pl.BlockSpec((1, HG_DV), lambda b, c: (0, 0)),
        ],
        out_specs=pl.BlockSpec((1, lc, W), lambda b, c: (b, c, 0)),
        out_shape=jax.ShapeDtypeStruct((B, S, W), F32),
        scratch_shapes=[pltpu.VMEM((HG_HEADS, HG_DV, HG_DK), F32)] + [scr() for _ in range(8)],
        compiler_params=_params("parallel", "arbitrary"),
        name="hgrn2",
    )(zh, zh, zh, zh, lb_table, gain)


def _mix_out_body(a_ref, r_ref, x_ref, gm_ref, wa_ref, wr_ref, gp_ref, o_ref):
    an = _rms(a_ref[...], gm_ref[...]).astype(BF16)
    y = jnp.dot(an, wa_ref[...], preferred_element_type=F32)
    y = y + jnp.dot(r_ref[...].astype(BF16), wr_ref[...], preferred_element_type=F32)
    o_ref[...] = x_ref[...] + _rms(y, gp_ref[...])


def _mix_out(a2d, r2d, x2d, gm, wa, wr, gp, tm):
    N, D = x2d.shape
    return pl.pallas_call(
        _mix_out_body,
        grid=(N // tm,),
        in_specs=[
            pl.BlockSpec((tm, a2d.shape[1]), lambda i: (i, 0)),
            pl.BlockSpec((tm, r2d.shape[1]), lambda i: (i, 0)),
            pl.BlockSpec((tm, D), lambda i: (i, 0)),
            _const_spec(gm.shape), _const_spec(wa.shape), _const_spec(wr.shape), _const_spec(gp.shape),
        ],
        out_specs=pl.BlockSpec((tm, D), lambda i: (i, 0)),
        out_shape=jax.ShapeDtypeStruct((N, D), F32),
        compiler_params=_params("parallel"),
        name="mix_out",
    )(a2d, r2d, x2d, gm, wa, wr, gp)


def _xattn_body(h_ref, gpre_ref, wq_ref, kv_ref, wo_ref, gpost_ref, o_ref, oh_ref, *, q_scale):
    h = h_ref[...]
    D = h.shape[-1]
    dh = D // X_HEADS
    xn = _rms(h, gpre_ref[...]).astype(BF16)
    qx = (jnp.dot(xn, wq_ref[...], preferred_element_type=F32) * q_scale).astype(BF16)
    for hd in range(X_HEADS):
        qh = qx[:, hd * dh:(hd + 1) * dh]
        kh = kv_ref[0, :, hd * dh:(hd + 1) * dh]
        vh = kv_ref[0, :, D + hd * dh:D + (hd + 1) * dh]
        s = lax.dot_general(qh, kh, (((1,), (1,)), ((), ())), preferred_element_type=F32)
        p = jnp.exp2(s - jnp.max(s, axis=-1, keepdims=True))
        l = jnp.sum(p, axis=-1, keepdims=True)
        oh = jnp.dot(p.astype(BF16), vh, preferred_element_type=F32) / l
        oh_ref[:, hd * dh:(hd + 1) * dh] = oh.astype(BF16)
    ox = jnp.dot(oh_ref[...], wo_ref[...], preferred_element_type=F32)
    o_ref[...] = h + _rms(ox, gpost_ref[...])


def _xattn(h2d, gpre, wq, kvm, wo, gpost, tm, tiles_per_batch):
    N, D = h2d.shape
    M = kvm.shape[1]
    q_scale = ((D // X_HEADS) ** -0.5) * LOG2E
    return pl.pallas_call(
        functools.partial(_xattn_body, q_scale=q_scale),
        grid=(N // tm,),
        in_specs=[
            pl.BlockSpec((tm, D), lambda i: (i, 0)),
            _const_spec(gpre.shape), _const_spec(wq.shape),
            pl.BlockSpec((1, M, 2 * D), lambda i: (i // tiles_per_batch, 0, 0)),
            _const_spec(wo.shape), _const_spec(gpost.shape),
        ],
        out_specs=pl.BlockSpec((tm, D), lambda i: (i, 0)),
        out_shape=jax.ShapeDtypeStruct((N, D), F32),
        scratch_shapes=[pltpu.VMEM((tm, D), BF16)],
        compiler_params=_params("parallel"),
        name="xattn",
    )(h2d, gpre, wq, kvm, wo, gpost)


def _gelu_tanh(x):
    return x * (0.5 * (1.0 + jnp.tanh(math.sqrt(2.0 / math.pi) * (x + 0.044715 * (x * x * x)))))


def _ffn_up_body(h_ref, gpre_ref, wg_ref, wv_ref, cwg_ref, cwv_ref, cbg_ref, cbv_ref,
                 act_ref, xn_ref, cg_ref, cv_ref, bg0_ref, bv0_ref, bg1_ref, bv1_ref,
                 *, tm, nj, tiles_per_batch):
    i = pl.program_id(0)
    j = pl.program_id(1)
    first = (i % tiles_per_batch) == 0
    slots = ((bg0_ref, bv0_ref), (bg1_ref, bv1_ref))

    def project(slot):
        for w_ref, carry_ref, buf_ref in zip((wg_ref, wv_ref), (cg_ref, cv_ref), slots[slot]):
            u = jnp.dot(xn_ref[...], w_ref[...], preferred_element_type=F32)
            buf_ref[0:CONV_PAD, :] = jnp.where(first, 0.0, carry_ref[j])
            buf_ref[CONV_PAD:CONV_PAD + tm, :] = u
            carry_ref[j] = u[tm - CONV_PAD:, :]

    def conv(buf_ref, cw, cb, r):
        base = CONV_PAD + r
        return (cw[0:1, :] * buf_ref[base - 2:base - 2 + FF_CHUNK, :]
                + cw[1:2, :] * buf_ref[base - 1:base - 1 + FF_CHUNK, :]
                + cw[2:3, :] * buf_ref[base:base + FF_CHUNK, :] + cb)

    def finish(slot):
        bg_ref, bv_ref = slots[slot]
        cwg = cwg_ref[...]
        cwv = cwv_ref[...]
        cbg = cbg_ref[...]
        cbv = cbv_ref[...]
        for r in range(0, tm, FF_CHUNK):
            gate = conv(bg_ref, cwg, cbg, r)
            val = conv(bv_ref, cwv, cbv, r)
            act_ref[r:r + FF_CHUNK, :] = (_gelu_tanh(gate) * val).astype(BF16)

    @pl.when(j == 0)
    def _():
        @pl.when(i == 0)
        def _():
            cg_ref[...] = jnp.zeros(cg_ref.shape, F32)
            cv_ref[...] = jnp.zeros(cv_ref.shape, F32)

        xn_ref[...] = _rms(h_ref[...], gpre_ref[...]).astype(BF16)
        project(0)

    for parity in (0, 1):
        @pl.when((j > 0) & (j < nj) & (j % 2 == parity))
        def _():
            finish(1 - parity)
            project(parity)

    @pl.when(j == nj)
    def _():
        finish((nj - 1) % 2)


def _ffn_up(h2d, gpre, wg, wv, cwg, cwv, cbg, cbv, tm, tf, tiles_per_batch):
    N, D = h2d.shape
    F = wg.shape[1]
    nj = F // tf
    cur = lambda i, j: (0, jnp.minimum(j, nj - 1))
    prev = lambda i, j: (0, jnp.maximum(j - 1, 0))
    buf = lambda: pltpu.VMEM((CONV_PAD + tm, tf), F32)
    return pl.pallas_call(
        functools.partial(_ffn_up_body, tm=tm, nj=nj, tiles_per_batch=tiles_per_batch),
        grid=(N // tm, nj + 1),
        in_specs=[
            pl.BlockSpec((tm, D), lambda i, j: (i, 0)),
            _const_spec(gpre.shape),
            pl.BlockSpec((D, tf), cur),
            pl.BlockSpec((D, tf), cur),
            pl.BlockSpec((CONV_WIDTH, tf), prev), pl.BlockSpec((CONV_WIDTH, tf), prev),
            pl.BlockSpec((1, tf), prev), pl.BlockSpec((1, tf), prev),
        ],
        out_specs=pl.BlockSpec((tm, tf), lambda i, j: (i, jnp.maximum(j - 1, 0))),
        out_shape=jax.ShapeDtypeStruct((N, F), BF16),
        scratch_shapes=[
            pltpu.VMEM((tm, D), BF16),
            pltpu.VMEM((nj, CONV_PAD, tf), F32),
            pltpu.VMEM((nj, CONV_PAD, tf), F32),
            buf(), buf(), buf(), buf(),
        ],
        compiler_params=_params("arbitrary", "arbitrary"),
        name="ffn_up",
    )(h2d, gpre, wg, wv, cwg, cwv, cbg, cbv)


def _ffn_down_body(act_ref, h_ref, wd_ref, gpost_ref, o_ref):
    y = jnp.dot(act_ref[...], wd_ref[...], preferred_element_type=F32)
    o_ref[...] = h_ref[...] + _rms(y, gpost_ref[...])


def _ffn_down(act, h2d, wd, gpost, tm):
    N, D = h2d.shape
    F = act.shape[1]
    return pl.pallas_call(
        _ffn_down_body,
        grid=(N // tm,),
        in_specs=[
            pl.BlockSpec((tm, F), lambda i: (i, 0)),
            pl.BlockSpec((tm, D), lambda i: (i, 0)),
            _const_spec(wd.shape), _const_spec(gpost.shape),
        ],
        out_specs=pl.BlockSpec((tm, D), lambda i: (i, 0)),
        out_shape=jax.ShapeDtypeStruct((N, D), F32),
        compiler_params=_params("parallel"),
        name="ffn_down",
    )(act, h2d, wd, gpost)


def kernel(x, mem, w_in, q_norm, w_uq, kv_norm, w_ukv, mla_out_norm, hgrn_lb, hgrn_out_norm, w_out,
           ln_mix_pre, ln_mix_post, ln_x_pre, ln_x_post, mem_norm, w_xq, w_xk, w_xv, w_xo,
           ln_ffn_pre, ln_ffn_post, w_up, conv_w, conv_b, w_down):
    B, S, D = x.shape
    M = mem.shape[1]
    depth = w_in.shape[0]
    N = B * S
    tm = min(ROW_TILE, S)
    blk = min(ATT_BLK, S)
    bq = min(ATT_QBLK, S)
    lc = min(HG_BLK, S)
    assert S % tm == 0 and S % blk == 0 and S % bq == 0 and bq % blk == 0 and blk % CHUNK == 0
    assert S % lc == 0 and lc % HG_SUB == 0
    tiles_per_batch = S // tm
    d_ff = w_down.shape[1]
    tf = FF_TILE
    tmf = min(FF_ROWS, S)
    assert d_ff % tf == 0 and S % tmf == 0 and tmf % FF_CHUNK == 0
    n_mla_in = MLA_Q_RANK + MLA_KV_RANK + MLA_ROPE

    pos = jnp.arange(S, dtype=F32)
    inv_freq = 1.0 / (ROPE_THETA ** (jnp.arange(0, MLA_ROPE, 2, dtype=F32) / MLA_ROPE))
    ang = pos[:, None] * inv_freq[None, :]
    cos, sin = jnp.cos(ang), jnp.sin(ang)
    cosT, sinT = cos.T, sin.T

    row = lambda v: v.reshape(1, -1).astype(F32)
    h = x
    for l in range(depth):
        w_mla = w_in[l][:, :n_mla_in].astype(BF16)
        w_hg = w_in[l][:, n_mla_in:].astype(BF16)
        w_uqT = w_uq[l].T.astype(BF16)
        w_ukv_h = w_ukv[l].reshape(MLA_KV_RANK, MLA_HEADS, MLA_NOPE + MLA_V)
        w_uk = w_ukv_h[:, :, :MLA_NOPE].reshape(MLA_KV_RANK, MLA_HEADS * MLA_NOPE).astype(BF16)
        w_uvT = w_ukv_h[:, :, MLA_NOPE:].reshape(MLA_KV_RANK, MLA_HEADS * MLA_V).T.astype(BF16)
        w_out_a = w_out[l][:MLA_WIDTH].astype(BF16)
        w_out_r = w_out[l][MLA_WIDTH:].astype(BF16)
        w_xkv = jnp.concatenate([w_xk[l], w_xv[l]], axis=1).astype(BF16)
        w_up_g = w_up[l][:, :d_ff].astype(BF16)
        w_up_v = w_up[l][:, d_ff:].astype(BF16)

        qT, k, vT = _mla_prep(h, row(ln_mix_pre[l]), w_mla, row(q_norm[l]), w_uqT, row(kv_norm[l]),
                              w_uk, w_uvT, cos, sin, cosT, sinT, blk)
        a = _attention(qT, k, vT, bq, blk)
        zh = _norm_matmul(h.reshape(N, D), row(ln_mix_pre[l]), w_hg, F32, tm, w_hg.shape[1], "hgrn_in")
        r = _hgrn(zh.reshape(B, S, -1), hgrn_lb.astype(F32), row(hgrn_out_norm[l]), l, lc)
        h2d = _mix_out(a.reshape(N, -1), r.reshape(N, -1), h.reshape(N, D), row(mla_out_norm[l]),
                       w_out_a, w_out_r, row(ln_mix_post[l]), tm)

        kvm = _norm_matmul(mem.reshape(B * M, D), row(mem_norm[l]), w_xkv, BF16, min(MEM_ROWS, B * M),
                           MEM_COLS, "mem_kv")
        h2d = _xattn(h2d, row(ln_x_pre[l]), w_xq[l].astype(BF16), kvm.reshape(B, M, 2 * D),
                     w_xo[l].astype(BF16), row(ln_x_post[l]), tm, tiles_per_batch)

        act = _ffn_up(h2d, row(ln_ffn_pre[l]), w_up_g, w_up_v, conv_w[l][:, :d_ff], conv_w[l][:, d_ff:],
                      row(conv_b[l][:d_ff]), row(conv_b[l][d_ff:]), tmf, tf, S // tmf)
        h2d = _ffn_down(act, h2d, w_down[l].astype(BF16), row(ln_ffn_post[l]), min(FF_DOWN_ROWS, S))
        h = h2d.reshape(B, S, D)
    return h
```
